```python
import math
import jax
import jax.numpy as jnp
from jax import lax
import numpy as np

D_MODEL = 1024
BATCH = 4
SEQ = 4096
DEPTH = 4
DEC_BATCH = 32
DEC_SEQ = 8
PAST_LEN = 8192
PAGE_SIZE = 128

HEAD_DIM = 64
N_MIXERS = 4
HEADS_PER_MIXER = D_MODEL // HEAD_DIM // N_MIXERS
BRANCH_WIDTH = HEADS_PER_MIXER * HEAD_DIM
D_MIX = N_MIXERS * BRANCH_WIDTH
MOBA_BLOCK = 256
MOBA_TOPK = 3
DIFF_HEAD_DIM = HEAD_DIM // 2
NSA_GROUPS = 2
NSA_BLOCK = 64
NSA_TOPN = 16
NSA_WINDOW = 512
NSA_PHI_HIDDEN = 128
NSA_FORCE_SCORE = 1.0e4
Q_BLOCK = 128
GATHER_Q_BLOCK = 64
RMS_EPS = 1e-6
N_ALIBI_HEADS = 3 * HEADS_PER_MIXER
KV_C = NSA_GROUPS * HEAD_DIM
IN_SPLITS = (
    BRANCH_WIDTH, BRANCH_WIDTH, BRANCH_WIDTH, BRANCH_WIDTH,
    BRANCH_WIDTH, BRANCH_WIDTH, BRANCH_WIDTH, BRANCH_WIDTH,
    BRANCH_WIDTH, 6 * KV_C, 3 * HEADS_PER_MIXER, BRANCH_WIDTH,
    BRANCH_WIDTH, BRANCH_WIDTH, BRANCH_WIDTH, HEADS_PER_MIXER, BRANCH_WIDTH,
)
N_IN = sum(IN_SPLITS)

kernel_name = 'hymba_moba_diff_nsa_fox_step'


def rmsnorm(x, g):
    xf = x.astype(jnp.float32)
    y = xf * lax.rsqrt(jnp.mean(xf * xf, axis=-1, keepdims=True) + RMS_EPS)
    return (y * g.astype(jnp.float32)).astype(x.dtype)


def masked_softmax(logits, mask):
    l = jnp.where(mask, logits, -jnp.inf)
    m = jnp.max(l, axis=-1, keepdims=True)
    m = jnp.where(jnp.isfinite(m), m, 0.0)
    e = jnp.where(mask, jnp.exp(l - m), 0.0)
    s = jnp.sum(e, axis=-1, keepdims=True)
    return e / jnp.where(s > 0, s, 1.0)


def alibi_slopes():
    i = jnp.arange(1, N_ALIBI_HEADS + 1, dtype=jnp.float32)
    s = jnp.exp2(-8.0 * i / N_ALIBI_HEADS)
    return s.reshape(HEADS_PER_MIXER, 3).T


def sweep_queries(fn, q_block, qpos, *q_arrays):
    T = qpos.shape[0]
    if T <= q_block or T % q_block != 0:
        return fn(qpos, *q_arrays)
    nb = T // q_block
    blocks = tuple(a.reshape(a.shape[0], nb, q_block, *a.shape[2:]).swapaxes(0, 1) for a in q_arrays)
    out = lax.map(lambda args: fn(*args), (qpos.reshape(nb, q_block),) + blocks)
    out = out.swapaxes(0, 1)
    return out.reshape(out.shape[0], T, *out.shape[3:])


def pad_blocks(x, block):
    L = x.shape[1]
    nb = -(-L // block)
    x = jnp.pad(x, [(0, 0), (0, nb * block - L)] + [(0, 0)] * (x.ndim - 2))
    return x.reshape(x.shape[0], nb, block, *x.shape[2:])


def gather_pages(pool, page_table):
    g = pool[page_table]
    return g.reshape(g.shape[0], g.shape[1] * g.shape[2], *g.shape[3:])


def moba_attend(qpos, q, k_blocks, v_blocks, k_means, slopes):
    B, QB, H, D = q.shape
    NB, BS = k_blocks.shape[2], k_blocks.shape[3]
    qblk = qpos // BS
    gate = jnp.einsum('bqhd,bhnd->bqhn', q, k_means).astype(jnp.float32)
    fully_past = jnp.arange(NB)[None, :] < qblk[:, None]
    gate = jnp.where(fully_past[None, :, None, :], gate, -jnp.inf)
    top_s, top_i = lax.top_k(gate, min(MOBA_TOPK, NB))
    own = jnp.broadcast_to(qblk[None, :, None, None], (B, QB, H, 1)).astype(top_i.dtype)
    sel = jnp.concatenate([top_i, own], axis=-1)
    ok = jnp.concatenate([jnp.isfinite(top_s), jnp.ones((B, QB, H, 1), dtype=bool)], axis=-1)
    bi = jnp.arange(B)[:, None, None, None]
    hi = jnp.arange(H)[None, None, :, None]
    kg = k_blocks[bi, hi, sel]
    vg = v_blocks[bi, hi, sel]
    J = sel.shape[-1]
    kpos = sel[..., None] * BS + jnp.arange(BS)
    dist = (qpos[None, :, None, None, None] - kpos).astype(jnp.float32)
    logits = jnp.einsum('bqhd,bqhjsd->bqhjs', q, kg).astype(jnp.float32) / math.sqrt(D)
    logits = logits - slopes[None, None, :, None, None] * dist
    mask = ok[..., None] & (dist >= 0)
    p = masked_softmax(logits.reshape(B, QB, H, J * BS), mask.reshape(B, QB, H, J * BS))
    return jnp.einsum('bqhjs,bqhjsd->bqhd', p.reshape(B, QB, H, J, BS).astype(vg.dtype), vg)


def diff_attend(qpos, q, k, v, kpos, slopes, lam):
    dh = q.shape[-1]
    logits = jnp.einsum('bqhcd,bkhcd->bhcqk', q, k).astype(jnp.float32) / math.sqrt(dh)
    dist = (qpos[:, None] - kpos[None, :]).astype(jnp.float32)
    logits = logits - slopes[None, :, None, None, None] * dist
    p = masked_softmax(logits, dist >= 0)
    a = p[:, :, 0] - lam * p[:, :, 1]
    return jnp.einsum('bhqk,bkhd->bqhd', a.astype(v.dtype), v)


def fox_attend(qpos, q, cq, k, v, ck_t, kpos):
    D = q.shape[-1]
    logits = jnp.einsum('bqhd,bkhd->bhqk', q, k).astype(jnp.float32) / math.sqrt(D)
    logits = logits + cq.transpose(0, 2, 1)[..., None] - ck_t[:, :, None, :]
    p = masked_softmax(logits, kpos[None, :] <= qpos[:, None])
    return jnp.einsum('bhqk,bkhd->bqhd', p.astype(v.dtype), v)


def nsa_compress(rows, pe, w1, w2):
    blocks = pad_blocks(rows, NSA_BLOCK) + pe[None, None, :, None, :].astype(rows.dtype)
    B, NB, SB, G, D = blocks.shape
    flat = blocks.transpose(0, 1, 3, 2, 4).reshape(B, NB, G, SB * D)
    return jax.nn.silu(flat @ w1) @ w2


def nsa_attend(qpos, q, gates, kc, vc, ks_blocks, vs_blocks, kwin_pad, vwin_pad, win_start, slopes):
    B, QB, H, D = q.shape
    G = kc.shape[2]
    R = H // G
    NB, SB = ks_blocks.shape[2], ks_blocks.shape[3]
    qg = q.reshape(B, QB, G, R, D)
    sl = slopes.reshape(G, R)
    scale = 1.0 / math.sqrt(D)
    t = qpos
    cend = jnp.arange(NB) * SB + (SB - 1)
    cdist = (t[:, None] - cend[None, :]).astype(jnp.float32)
    lc = jnp.einsum('bqgrd,bngd->bqgrn', qg, kc).astype(jnp.float32) * scale
    lc = lc - sl[None, None, :, :, None] * cdist[None, :, None, None, :]
    p_cmp = masked_softmax(lc, (cdist >= 0)[None, :, None, None, :])
    o_cmp = jnp.einsum('bqgrn,bngd->bqgrd', p_cmp.astype(vc.dtype), vc)
    imp = jnp.sum(p_cmp, axis=3)
    blk = jnp.arange(NB)[None, :]
    cur = (t // SB)[:, None]
    forced = (blk == 0) | (blk == cur) | (blk == cur - 1)
    score = jnp.where(forced[None, :, None, :], NSA_FORCE_SCORE, imp)
    score = jnp.where((blk <= cur)[None, :, None, :], score, -jnp.inf)
    top_s, sel = lax.top_k(score, min(NSA_TOPN, NB))
    ok = jnp.isfinite(top_s)
    bi = jnp.arange(B)[:, None, None, None]
    gi = jnp.arange(G)[None, None, :, None]
    kg = ks_blocks[bi, gi, sel]
    vg = vs_blocks[bi, gi, sel]
    J = sel.shape[-1]
    kpos = sel[..., None] * SB + jnp.arange(SB)
    sdist = (t[None, :, None, None, None] - kpos).astype(jnp.float32)[:, :, :, None]
    ls = jnp.einsum('bqgrd,bqgjsd->bqgrjs', qg, kg).astype(jnp.float32) * scale
    ls = ls - sl[None, None, :, :, None, None] * sdist
    smask = ok[:, :, :, None, :, None] & (sdist >= 0)
    p_sel = masked_softmax(ls.reshape(B, QB, G, R, J * SB), smask.reshape(B, QB, G, 1, J * SB))
    o_sel = jnp.einsum('bqgrjs,bqgjsd->bqgrd', p_sel.reshape(B, QB, G, R, J, SB).astype(vg.dtype), vg)
    WL = NSA_WINDOW + QB
    start = t[0] - win_start
    kw = lax.dynamic_slice_in_dim(kwin_pad, start, WL, axis=1)
    vw = lax.dynamic_slice_in_dim(vwin_pad, start, WL, axis=1)
    wpos = t[0] - NSA_WINDOW + jnp.arange(WL)
    wdist = t[:, None] - wpos[None, :]
    wmask = (wpos[None, :] >= win_start) & (wdist >= 0) & (wdist < NSA_WINDOW)
    lw = jnp.einsum('bqgrd,bkgd->bqgrk', qg, kw).astype(jnp.float32) * scale
    lw = lw - sl[None, None, :, :, None] * wdist.astype(jnp.float32)[None, :, None, None, :]
    p_win = masked_softmax(lw, wmask[None, :, None, None, :])
    o_win = jnp.einsum('bqgrk,bkgd->bqgrd', p_win.astype(vw.dtype), vw)
    g = jax.nn.sigmoid(gates.astype(jnp.float32))
    out = (g[..., 0:1] * o_cmp.reshape(B, QB, H, D).astype(jnp.float32)
           + g[..., 1:2] * o_sel.reshape(B, QB, H, D).astype(jnp.float32)
           + g[..., 2:3] * o_win.reshape(B, QB, H, D).astype(jnp.float32))
    return out.astype(q.dtype)


def mixer_layer(h, past, t0, slopes, w_in, w_out, diff_lam, subln_g, lam_init, nsa_pe, phi_w1, phi_w2, fox_b):
    B, T, _ = h.shape
    H, D, G = HEADS_PER_MIXER, HEAD_DIM, NSA_GROUPS
    qpos = t0 + jnp.arange(T, dtype=jnp.int32)
    offsets = [int(o) for o in np.cumsum(IN_SPLITS)[:-1]]
    (a_q, a_k, a_v, a_g, b_q, b_k, b_v, b_g, c_q, c_kv, c_gl, c_g,
     d_q, d_k, d_v, d_f, d_g) = jnp.split(h @ w_in, offsets, axis=-1)
    a_new = jnp.stack([a_k, a_v], axis=2).reshape(B, T, 2, H, D)
    b_new = jnp.stack([b_k, b_v], axis=2).reshape(B, T, 2, H, D)
    c_six = c_kv.reshape(B, T, 6, G, D)
    c_new = c_six[:, :, :4]
    win_new = c_six[:, :, 4:]
    d_new = jnp.stack([d_k, d_v], axis=2).reshape(B, T, 2, H, D)
    logf_new = jax.nn.log_sigmoid((d_f + fox_b).astype(jnp.float32)).astype(h.dtype)
    if past is None:
        a_all, b_all, c_all, win_all, d_all, logf_all = a_new, b_new, c_new, win_new, d_new, logf_new
        win_start = t0
        win_keep = min(NSA_WINDOW, T)
    else:
        pa, pb, pc, pw, pd, plf = past
        a_all = jnp.concatenate([pa, a_new], axis=1)
        b_all = jnp.concatenate([pb, b_new], axis=1)
        c_all = jnp.concatenate([pc, c_new], axis=1)
        win_all = jnp.concatenate([pw, win_new], axis=1)
        d_all = jnp.concatenate([pd, d_new], axis=1)
        logf_all = jnp.concatenate([plf, logf_new], axis=1)
        win_start = t0 - pw.shape[1]
        win_keep = pw.shape[1]
    L = a_all.shape[1]
    kpos = jnp.arange(L, dtype=jnp.int32)

    kb = pad_blocks(a_all[:, :, 0], MOBA_BLOCK).transpose(0, 3, 1, 2, 4)
    vb = pad_blocks(a_all[:, :, 1], MOBA_BLOCK).transpose(0, 3, 1, 2, 4)
    kmean = jnp.mean(kb.astype(jnp.float32), axis=3).astype(kb.dtype)
    o_a = sweep_queries(lambda p, q: moba_attend(p, q, kb, vb, kmean, slopes[0]),
                        GATHER_Q_BLOCK, qpos, a_q.reshape(B, T, H, D))

    lam_v = diff_lam.astype(jnp.float32)
    lam = (jnp.exp(jnp.sum(lam_v[0] * lam_v[1])) - jnp.exp(jnp.sum(lam_v[2] * lam_v[3])) + lam_init)
    bk = b_all[:, :, 0].reshape(B, L, H, 2, DIFF_HEAD_DIM)
    bv = b_all[:, :, 1]
    o_b = sweep_queries(lambda p, q: diff_attend(p, q, bk, bv, kpos, slopes[1], lam),
                        Q_BLOCK, qpos, b_q.reshape(B, T, H, 2, DIFF_HEAD_DIM))
    o_b = rmsnorm(o_b, subln_g) * (1.0 - lam_init)

    kc = nsa_compress(c_all[:, :, 0], nsa_pe[0], phi_w1[0], phi_w2[0])
    vc = nsa_compress(c_all[:, :, 1], nsa_pe[1], phi_w1[1], phi_w2[1])
    ksb = pad_blocks(c_all[:, :, 2], NSA_BLOCK).transpose(0, 3, 1, 2, 4)
    vsb = pad_blocks(c_all[:, :, 3], NSA_BLOCK).transpose(0, 3, 1, 2, 4)
    pad_w = [(0, 0), (NSA_WINDOW, 0), (0, 0), (0, 0)]
    kwp = jnp.pad(win_all[:, :, 0], pad_w)
    vwp = jnp.pad(win_all[:, :, 1], pad_w)
    o_c = sweep_queries(lambda p, q, gt: nsa_attend(p, q, gt, kc, vc, ksb, vsb, kwp, vwp, win_start, slopes[2]),
                        GATHER_Q_BLOCK, qpos, c_q.reshape(B, T, H, D), c_gl.reshape(B, T, H, 3))

    cum = jnp.cumsum(logf_all.astype(jnp.float32), axis=1)
    cq = cum[:, L - T:]
    ck_t = cum.transpose(0, 2, 1)
    dk = d_all[:, :, 0]
    dv = d_all[:, :, 1]
    o_d = sweep_queries(lambda p, q, c: fox_attend(p, q, c, dk, dv, ck_t, kpos),
                        Q_BLOCK, qpos, d_q.reshape(B, T, H, D), cq)

    y = jnp.concatenate([o_a.reshape(B, T, BRANCH_WIDTH) * jax.nn.silu(a_g),
                         o_b.reshape(B, T, BRANCH_WIDTH) * jax.nn.silu(b_g),
                         o_c.reshape(B, T, BRANCH_WIDTH) * jax.nn.silu(c_g),
                         o_d.reshape(B, T, BRANCH_WIDTH) * jax.nn.silu(d_g)], axis=-1) @ w_out
    new_state = (a_new, b_new, c_new, win_all[:, win_all.shape[1] - win_keep:], d_new, logf_new)
    return y, new_state


def setup_inputs(seed: int = 0) -> dict:
    key = jax.random.key(seed)
    ks = jax.random.split(key, 20)
    f32 = jnp.float32
    H, D, G = HEADS_PER_MIXER, HEAD_DIM, NSA_GROUPS
    n_pages = PAST_LEN // PAGE_SIZE
    n_used = DEC_BATCH * n_pages
    n_pool = n_used + max(1, n_used // 4)
    win_c = min(NSA_WINDOW, PAST_LEN)

    def nrm(k, s):
        return jax.random.normal(k, s, f32)

    return {
        'x_prompt': nrm(ks[0], (BATCH, SEQ, D_MODEL)),
        'x_sample': nrm(ks[1], (DEC_BATCH, DEC_SEQ, D_MODEL)),
        'cache_a_kv': nrm(ks[2], (DEPTH, n_pool, PAGE_SIZE, 2, H, D)),
        'cache_b_kv': nrm(ks[3], (DEPTH, n_pool, PAGE_SIZE, 2, H, D)),
        'cache_c_kv': nrm(ks[4], (DEPTH, n_pool, PAGE_SIZE, 4, G, D)),
        'cache_c_win': nrm(ks[5], (DEPTH, DEC_BATCH, win_c, 2, G, D)),
        'cache_d_kv': nrm(ks[6], (DEPTH, n_pool, PAGE_SIZE, 2, H, D)),
        'cache_d_logf': jax.nn.log_sigmoid(2.0 + nrm(ks[7], (DEPTH, n_pool, PAGE_SIZE, H))),
        'page_table': jax.random.permutation(ks[8], n_pool)[:n_used].reshape(DEC_BATCH, n_pages).astype(jnp.int32),
        'norm_g': 1.0 + 0.02 * nrm(ks[9], (DEPTH, D_MODEL)),
        'w_in': nrm(ks[10], (DEPTH, D_MODEL, N_IN)) * D_MODEL ** -0.5,
        'w_out': nrm(ks[11], (DEPTH, D_MIX, D_MODEL)) * D_MIX ** -0.5,
        'diff_lambda': 0.1 * nrm(ks[12], (DEPTH, 4, DIFF_HEAD_DIM)),
        'diff_subln_g': 1.0 + 0.02 * nrm(ks[13], (DEPTH, HEAD_DIM)),
        'nsa_pe': 0.5 * nrm(ks[14], (DEPTH, 2, NSA_BLOCK, HEAD_DIM)),
        'nsa_phi_w1': nrm(ks[15], (DEPTH, 2, NSA_BLOCK * HEAD_DIM, NSA_PHI_HIDDEN)) * (NSA_BLOCK * HEAD_DIM) ** -0.5,
        'nsa_phi_w2': nrm(ks[16], (DEPTH, 2, NSA_PHI_HIDDEN, HEAD_DIM)) * NSA_PHI_HIDDEN ** -0.5,
        'fox_bias': 2.0 + 0.5 * nrm(ks[17], (DEPTH, H)),
        'final_g': 1.0 + 0.02 * nrm(ks[18], (D_MODEL,)),
    }


def reference(x_prompt, x_sample, cache_a_kv, cache_b_kv, cache_c_kv, cache_c_win, cache_d_kv, cache_d_logf,
              page_table, norm_g, w_in, w_out, diff_lambda, diff_subln_g, nsa_pe, nsa_phi_w1, nsa_phi_w2,
              fox_bias, final_g):
    slopes = alibi_slopes()
    past_len = page_table.shape[1] * PAGE_SIZE
    xp, xs = x_prompt, x_sample
    new_p = [[] for _ in range(6)]
    new_s = [[] for _ in range(6)]
    for l in range(DEPTH):
        lam_init = 0.8 - 0.6 * math.exp(-0.3 * l)
        lp = (slopes, w_in[l], w_out[l], diff_lambda[l], diff_subln_g[l], lam_init,
              nsa_pe[l], nsa_phi_w1[l], nsa_phi_w2[l], fox_bias[l])
        yp, sp = mixer_layer(rmsnorm(xp, norm_g[l]), None, 0, *lp)
        past = (gather_pages(cache_a_kv[l], page_table), gather_pages(cache_b_kv[l], page_table),
                gather_pages(cache_c_kv[l], page_table), cache_c_win[l],
                gather_pages(cache_d_kv[l], page_table), gather_pages(cache_d_logf[l], page_table))
        ys, ss = mixer_layer(rmsnorm(xs, norm_g[l]), past, past_len, *lp)
        xp = xp + yp
        xs = xs + ys
        for i in range(6):
            new_p[i].append(sp[i])
            new_s[i].append(ss[i])
    y_prompt = rmsnorm(xp, final_g)
    y_sample = rmsnorm(xs, final_g)
    a_kv_p, b_kv_p, c_kv_p, c_win_p, d_kv_p, d_logf_p = [jnp.stack(s, axis=0) for s in new_p]
    a_kv_s, b_kv_s, c_kv_s, c_win_s, d_kv_s, d_logf_s = [jnp.stack(s, axis=0) for s in new_s]
    return (y_prompt, y_sample, a_kv_p, a_kv_s, b_kv_p, b_kv_s, c_kv_p, c_kv_s,
            c_win_p, c_win_s, d_kv_p, d_kv_s, d_logf_p, d_logf_s)
```

```python
import functools
import math

import numpy as np
import jax
import jax.numpy as jnp
from jax import lax
from jax.experimental import pallas as pl
from jax.experimental.pallas import tpu as pltpu

F32 = jnp.float32
BF16 = jnp.bfloat16
HIGHEST = lax.Precision.HIGHEST

D_MODEL = 1024
HEAD_DIM = 64
N_HEADS = 4
BRANCH = N_HEADS * HEAD_DIM
MOBA_BLOCK = 256
MOBA_TOPK = 3
DIFF_DIM = HEAD_DIM // 2
NSA_GROUPS = 2
NSA_BLOCK = 64
NSA_TOPN = 16
NSA_WINDOW = 512
NSA_HIDDEN = 128
NSA_FORCE_SCORE = 1.0e4
RMS_EPS = 1e-6
LOG2E = 1.4426950408889634
NEG = -1.0e30
LANES = 128
TILE = 256
PAGES_PER_STEP = 8
VMEM_LIMIT = 56 * 1024 * 1024

C_HEAD_ORDER = (0, 2, 1, 3)

N_IN_ORIG = 4368
COL_Q4 = 0
COL_G4 = 1024
COL_AKV = 2048
COL_BKV = 2560
COL_CKV = 3072
COL_CWIN = 3584
COL_DKV = 3840
COL_SMALL = 4352
N_IN_PAD = 4480
SMALL_F = 12


def _alibi_slopes():
    i = np.arange(1, 13, dtype=np.float32)
    s = np.exp2(np.float32(-8.0) * i / np.float32(12.0)).astype(np.float32)
    return s.reshape(N_HEADS, 3).T


def _in_perm():
    a = np.arange
    hp = np.array(C_HEAD_ORDER)
    cperm = (hp[:, None] * HEAD_DIM + a(HEAD_DIM)[None, :]).reshape(-1)
    cols = [
        a(0, 256), a(1024, 1280), 2048 + cperm, a(3340, 3596),
        a(768, 1024), a(1792, 2048), 3084 + cperm, a(4112, 4368),
        a(256, 768), a(1280, 1792), a(2304, 2816), a(2816, 3072), a(3596, 4108),
        a(3072, 3084), a(4108, 4112),
    ]
    return np.concatenate(cols)


def _dot_nt(a, b, precision=None):
    return lax.dot_general(a, b, (((1,), (1,)), ((), ())), preferred_element_type=F32, precision=precision)


def _dot(a, b, precision=None):
    return jnp.dot(a, b, preferred_element_type=F32, precision=precision)


def _top_select(score, k, lane_f):
    sel = jnp.zeros(score.shape, F32)
    cur = score
    for _ in range(k):
        mx = jnp.max(cur, axis=1, keepdims=True)
        idx = jnp.min(jnp.where(cur == mx, lane_f, 1.0e9), axis=1, keepdims=True)
        pick = lane_f == idx
        sel = jnp.where(pick & (mx > 0.5 * NEG), 1.0, sel)
        cur = jnp.where(pick, 3.0 * NEG, cur)
    return sel


def _online(s, m, l):
    m_new = jnp.maximum(m, jnp.max(s, axis=1, keepdims=True))
    alpha = jnp.exp2(m - m_new)
    p = jnp.exp2(s - m_new)
    return m_new, alpha, p, alpha * l + jnp.sum(p, axis=1, keepdims=True)


def _proj_kernel(x_ref, g_ref, w_ref, fb_ref, q4_ref, g4_ref, akv_ref, bkv_ref, ckv_ref, cwin_ref, dkv_ref,
                 small_ref, logf_ref, cum_ref, carry_ref, *, tm, seg_len):
    i = pl.program_id(0)
    x = x_ref[...]
    ms = jnp.mean(x * x, axis=-1, keepdims=True)
    h = ((x * lax.rsqrt(ms + RMS_EPS)) * g_ref[...]).astype(BF16)

    def mm(c0, c1):
        return _dot(h, w_ref[:, c0:c1])

    q4_ref[...] = mm(COL_Q4, COL_G4)
    g4_ref[...] = mm(COL_G4, COL_AKV)
    akv_ref[...] = mm(COL_AKV, COL_BKV)
    bkv_ref[...] = mm(COL_BKV, COL_CKV)
    ckv_ref[...] = mm(COL_CKV, COL_CWIN)
    cwin_ref[...] = mm(COL_CWIN, COL_DKV)
    dkv_ref[...] = mm(COL_DKV, COL_SMALL)
    small = mm(COL_SMALL, N_IN_PAD)
    small_ref[...] = small
    z = small + fb_ref[...]
    logf = jnp.minimum(z, 0.0) - jnp.log(1.0 + jnp.exp(-jnp.abs(z)))
    logf_ref[...] = logf
    r = lax.broadcasted_iota(jnp.int32, (tm, tm), 0)
    c = lax.broadcasted_iota(jnp.int32, (tm, tm), 1)
    if seg_len >= tm:
        tiles_per_seq = seg_len // tm

        @pl.when(i % tiles_per_seq == 0)
        def _():
            carry_ref[...] = jnp.zeros_like(carry_ref)

        tri = (c <= r).astype(F32)
        cum = _dot(tri, logf, HIGHEST) + carry_ref[...]
        cum_ref[...] = cum
        carry_ref[...] = cum[tm - 1:tm, :]
    else:
        tri = ((c <= r) & ((r // seg_len) == (c // seg_len))).astype(F32)
        cum_ref[...] = _dot(tri, logf, HIGHEST)


def _project(x2, g_row, w_l, fb_row, seg_len):
    m = x2.shape[0]
    tm = TILE
    widths = (1024, 1024, 512, 512, 512, 256, 512, LANES, LANES, LANES)
    row = lambda w: pl.BlockSpec((tm, w), lambda i: (i, 0))
    const = lambda shape: pl.BlockSpec(shape, lambda i: (0, 0))
    return pl.pallas_call(
        functools.partial(_proj_kernel, tm=tm, seg_len=seg_len),
        grid=(m // tm,),
        in_specs=[row(D_MODEL), const((1, D_MODEL)), const((D_MODEL, N_IN_PAD)), const((1, LANES))],
        out_specs=[row(w) for w in widths],
        out_shape=[jax.ShapeDtypeStruct((m, w), F32) for w in widths],
        scratch_shapes=[pltpu.VMEM((1, LANES), F32)],
        compiler_params=pltpu.CompilerParams(dimension_semantics=("arbitrary",), vmem_limit_bytes=VMEM_LIMIT),
        name="proj",
    )(x2, g_row, w_l, fb_row)


def _out_kernel(x_ref, oa_ref, ob_ref, oc_ref, od_ref, g4_ref, w_ref, fg_ref, y_ref, *, final):
    g4 = g4_ref[...]
    gate = g4 / (1.0 + jnp.exp(-g4))
    o4 = jnp.concatenate([oa_ref[...], ob_ref[...], oc_ref[...], od_ref[...]], axis=1)
    y = x_ref[...] + _dot((o4 * gate).astype(BF16), w_ref[...])
    if final:
        ms = jnp.mean(y * y, axis=-1, keepdims=True)
        y = (y * lax.rsqrt(ms + RMS_EPS)) * fg_ref[...]
    y_ref[...] = y


def _out_project(x2, outs, g4, w_l, fg_row, final):
    m = x2.shape[0]
    tm = TILE
    row = lambda w: pl.BlockSpec((tm, w), lambda i: (i, 0))
    const = lambda shape: pl.BlockSpec(shape, lambda i: (0, 0))
    return pl.pallas_call(
        functools.partial(_out_kernel, final=final),
        grid=(m // tm,),
        in_specs=[row(D_MODEL)] + [row(BRANCH)] * 4 + [row(D_MODEL), const((D_MODEL, D_MODEL)), const((1, D_MODEL))],
        out_specs=row(D_MODEL),
        out_shape=jax.ShapeDtypeStruct((m, D_MODEL), F32),
        compiler_params=pltpu.CompilerParams(dimension_semantics=("arbitrary",), vmem_limit_bytes=VMEM_LIMIT),
        name="outproj",
    )(x2, *outs, g4, w_l, fg_row)


def _cast_rows(src_ref, dst_ref, n_rows):
    def body(t, _):
        sl = pl.ds(pl.multiple_of(t * TILE, TILE), TILE)
        dst_ref[sl, :] = src_ref[sl, :].astype(BF16)
        return 0
    lax.fori_loop(0, n_rows // TILE, body, 0)


def _flash_pass(qh, kb_ref, vb_ref, acc_ref, i, own_bias, past_bias, row_bias=None):
    r = lax.broadcasted_iota(jnp.int32, (TILE, TILE), 0)
    c = lax.broadcasted_iota(jnp.int32, (TILE, TILE), 1)
    own = pl.ds(pl.multiple_of(i * TILE, TILE), TILE)
    s = _dot_nt(qh, kb_ref[own, :]) + own_bias
    s = jnp.where(c <= r, s, NEG)
    m = jnp.max(s, axis=1, keepdims=True)
    p = jnp.exp2(s - m)
    l = jnp.sum(p, axis=1, keepdims=True)
    acc_ref[...] = _dot(p.astype(BF16), vb_ref[own, :])

    def body(j, carry):
        m, l = carry
        sl = pl.ds(pl.multiple_of(j * TILE, TILE), TILE)
        s = _dot_nt(qh, kb_ref[sl, :]) + past_bias(j)
        if row_bias is not None:
            s = s + row_bias(j)
        m, alpha, p, l = _online(s, m, l)
        acc_ref[...] = alpha * acc_ref[...] + _dot(p.astype(BF16), vb_ref[sl, :])
        return m, l

    m, l = lax.fori_loop(0, i, body, (m, l))
    return acc_ref[...] / l


def _moba_prompt_kernel(q_ref, k_ref, v_ref, o_ref, kb_ref, vb_ref, km_ref, acc_ref, *, seq, slopes):
    i = pl.program_id(1)
    nb = seq // MOBA_BLOCK

    @pl.when(i == 0)
    def _():
        _cast_rows(k_ref, kb_ref, seq)
        _cast_rows(v_ref, vb_ref, seq)
        km_ref[...] = jnp.zeros_like(km_ref)
        for n in range(nb):
            km_ref[n:n + 1, :] = jnp.mean(k_ref[n * MOBA_BLOCK:(n + 1) * MOBA_BLOCK, :], axis=0, keepdims=True)

    q = q_ref[...]
    lane_head = lax.broadcasted_iota(jnp.int32, (1, BRANCH), 1) // HEAD_DIM
    blk = lax.broadcasted_iota(jnp.int32, (1, LANES), 1)
    blk_f = blk.astype(F32)
    key_f = lax.broadcasted_iota(jnp.int32, (1, TILE), 1).astype(F32)
    scale = LOG2E / math.sqrt(HEAD_DIM)
    out = jnp.zeros((TILE, BRANCH), F32)
    for h in range(N_HEADS):
        mask_h = lane_head == h
        qf = jnp.where(mask_h, q, 0.0)
        gate = _dot_nt(qf, km_ref[...], HIGHEST)
        gate = jnp.where(blk < i, gate, NEG)
        sel_bias = (1.0 - _top_select(gate, MOBA_TOPK, blk_f)) * NEG
        qh = (qf * scale).astype(BF16)
        slope = float(slopes[h]) * LOG2E

        def past_bias(j, slope=slope):
            return slope * ((j - i).astype(F32) * float(TILE) + key_f)

        def row_bias(j, sel_bias=sel_bias):
            return jnp.sum(jnp.where(blk == j, sel_bias, 0.0), axis=1, keepdims=True)

        o = _flash_pass(qh, kb_ref, vb_ref, acc_ref, i, slope * key_f, past_bias, row_bias)
        out = jnp.where(mask_h, o, out)
    o_ref[...] = out


def _diff_prompt_kernel(q_ref, k_ref, v_ref, lam_ref, sg_ref, o_ref, kb_ref, vb_ref, acc_ref, *, seq, slopes, lam_init):
    i = pl.program_id(1)

    @pl.when(i == 0)
    def _():
        _cast_rows(k_ref, kb_ref, seq)
        _cast_rows(v_ref, vb_ref, seq)

    q = q_ref[...]
    lane = lax.broadcasted_iota(jnp.int32, (1, BRANCH), 1)
    key_f = lax.broadcasted_iota(jnp.int32, (1, TILE), 1).astype(F32)
    scale = LOG2E / math.sqrt(DIFF_DIM)
    lam = lam_ref[...]
    out = jnp.zeros((TILE, BRANCH), F32)
    for h in range(N_HEADS):
        slope = float(slopes[h]) * LOG2E

        def past_bias(j, slope=slope):
            return slope * ((j - i).astype(F32) * float(TILE) + key_f)

        parts = []
        for comp in range(2):
            mask_c = (lane // DIFF_DIM) == (2 * h + comp)
            qh = (jnp.where(mask_c, q, 0.0) * scale).astype(BF16)
            parts.append(_flash_pass(qh, kb_ref, vb_ref, acc_ref, i, slope * key_f, past_bias))
        mask_h = (lane // HEAD_DIM) == h
        o = jnp.where(mask_h, parts[0] - lam * parts[1], 0.0)
        ms = jnp.sum(o * o, axis=1, keepdims=True) * (1.0 / HEAD_DIM)
        o = (o * lax.rsqrt(ms + RMS_EPS)) * sg_ref[...] * (1.0 - lam_init)
        out = jnp.where(mask_h, o, out)
    o_ref[...] = out


def _fox_prompt_kernel(q_ref, k_ref, v_ref, ck_ref, o_ref, kb_ref, vb_ref, acc_ref, *, seq):
    i = pl.program_id(1)

    @pl.when(i == 0)
    def _():
        _cast_rows(k_ref, kb_ref, seq)
        _cast_rows(v_ref, vb_ref, seq)

    q = q_ref[...]
    lane_head = lax.broadcasted_iota(jnp.int32, (1, BRANCH), 1) // HEAD_DIM
    scale = LOG2E / math.sqrt(HEAD_DIM)
    own = pl.ds(pl.multiple_of(i * TILE, TILE), TILE)
    out = jnp.zeros((TILE, BRANCH), F32)
    for h in range(N_HEADS):
        mask_h = lane_head == h
        qh = (jnp.where(mask_h, q, 0.0) * scale).astype(BF16)
        ck_own = ck_ref[h:h + 1, own]
        c0 = ck_own[:, 0:1]

        def past_bias(j, c0=c0, h=h):
            sl = pl.ds(pl.multiple_of(j * TILE, TILE), TILE)
            return (c0 - ck_ref[h:h + 1, sl]) * LOG2E

        o = _flash_pass(qh, kb_ref, vb_ref, acc_ref, i, (c0 - ck_own) * LOG2E, past_bias)
        out = jnp.where(mask_h, o, out)
    o_ref[...] = out


def _prompt_call(kern, q4, lane_blk, kv, extras, extra_specs, extra_scratch, name):
    b, seq, _ = kv.shape
    nt = seq // TILE
    in_specs = [
        pl.BlockSpec((None, TILE, BRANCH), lambda bi, i: (bi, i, lane_blk)),
        pl.BlockSpec((None, seq, BRANCH), lambda bi, i: (bi, 0, 0)),
        pl.BlockSpec((None, seq, BRANCH), lambda bi, i: (bi, 0, 1)),
    ] + extra_specs
    return pl.pallas_call(
        kern,
        grid=(b, nt),
        in_specs=in_specs,
        out_specs=pl.BlockSpec((None, TILE, BRANCH), lambda bi, i: (bi, i, 0)),
        out_shape=jax.ShapeDtypeStruct((b, seq, BRANCH), F32),
        scratch_shapes=[pltpu.VMEM((seq, BRANCH), BF16), pltpu.VMEM((seq, BRANCH), BF16)] + extra_scratch
        + [pltpu.VMEM((TILE, BRANCH), F32)],
        compiler_params=pltpu.CompilerParams(dimension_semantics=("arbitrary", "arbitrary"),
                                             vmem_limit_bytes=VMEM_LIMIT),
        name=name,
    )(q4, kv, kv, *extras)


def _compress_kernel(x_ref, pe_ref, w1_ref, w2_ref, o_ref, *, nb, nbp):
    def body(s, acc):
        xs = x_ref[pl.ds(s, nb, stride=NSA_BLOCK), :] + pe_ref[pl.ds(s, 1), :]
        return acc + _dot(xs.astype(BF16), w1_ref[s])
    hid = lax.fori_loop(0, NSA_BLOCK, body, jnp.zeros((nb, NSA_GROUPS * NSA_HIDDEN), F32))
    hid = hid / (1.0 + jnp.exp(-hid))
    tok = _dot(hid.astype(BF16), w2_ref[...])
    if nbp > nb:
        tok = jnp.concatenate([tok, jnp.zeros((nbp - nb, LANES), F32)], axis=0)
    o_ref[...] = tok


def _compress(ckv, pe2, w1bd, w2bd, nb, nbp):
    b, length, _ = ckv.shape
    return pl.pallas_call(
        functools.partial(_compress_kernel, nb=nb, nbp=nbp),
        grid=(2, b),
        in_specs=[
            pl.BlockSpec((None, length, LANES), lambda c, bi: (bi, 0, c)),
            pl.BlockSpec((None, NSA_BLOCK, LANES), lambda c, bi: (c, 0, 0)),
            pl.BlockSpec((None, NSA_BLOCK, LANES, NSA_GROUPS * NSA_HIDDEN), lambda c, bi: (c, 0, 0, 0)),
            pl.BlockSpec((None, NSA_GROUPS * NSA_HIDDEN, LANES), lambda c, bi: (c, 0, 0)),
        ],
        out_specs=pl.BlockSpec((None, None, nbp, LANES), lambda c, bi: (bi, c, 0, 0)),
        out_shape=jax.ShapeDtypeStruct((b, 2, nbp, LANES), F32),
        compiler_params=pltpu.CompilerParams(dimension_semantics=("arbitrary", "arbitrary"),
                                             vmem_limit_bytes=VMEM_LIMIT),
        name="nsa_compress",
    )(ckv, pe2, w1bd, w2bd)


def _nsa_kernel(q_ref, sm_ref, tok_ref, sk_ref, sv_ref, wk_ref, wv_ref, eg_ref, o_ref,
                skb_ref, svb_ref, wkb_ref, wvb_ref, *, tq, sel_len, win_len, nbp, q_off, w_off, prompt, slopes):
    i = pl.program_id(1)

    @pl.when(i == 0)
    def _():
        _cast_rows(sk_ref, skb_ref, sel_len)
        _cast_rows(sv_ref, svb_ref, sel_len)
        _cast_rows(wk_ref, wkb_ref, win_len)
        _cast_rows(wv_ref, wvb_ref, win_len)

    scale = LOG2E / math.sqrt(HEAD_DIM)
    q = q_ref[...] * scale
    lane = lax.broadcasted_iota(jnp.int32, (1, LANES), 1)
    t0 = q_off + i * tq
    t_col = t0 + lax.broadcasted_iota(jnp.int32, (tq, 1), 0)
    t2 = jnp.concatenate([t_col, t_col], axis=0)
    r2 = t2 - t0
    kc = tok_ref[0].astype(BF16)
    vc = tok_ref[1].astype(BF16)
    blk = lax.broadcasted_iota(jnp.int32, (1, nbp), 1)
    blk_f = blk.astype(F32)
    cend = blk * NSA_BLOCK + (NSA_BLOCK - 1)
    key_i = lax.broadcasted_iota(jnp.int32, (1, TILE), 1)
    sig = 1.0 / (1.0 + jnp.exp(-sm_ref[...]))
    gates = [_dot(sig, eg_ref[br], HIGHEST) for br in range(3)]
    n_sel_tiles = (i + 1) if prompt else sel_len // TILE
    w_lo = jnp.maximum(i - 2, 0) if prompt else 0
    w_hi = (i + 1) if prompt else win_len // TILE

    res = []
    for g in range(NSA_GROUPS):
        mask_g = (lane // HEAD_DIM) == g
        qg = jnp.concatenate([jnp.where(mask_g, q[:, :LANES], 0.0), jnp.where(mask_g, q[:, LANES:], 0.0)], axis=0)
        qgb = qg.astype(BF16)
        h_a, h_b = C_HEAD_ORDER[g], C_HEAD_ORDER[2 + g]
        slope2 = jnp.concatenate([jnp.full((tq, 1), float(slopes[h_a]) * LOG2E, F32),
                                  jnp.full((tq, 1), float(slopes[h_b]) * LOG2E, F32)], axis=0)
        cdist = t2 - cend
        cmask = cdist >= 0
        s = _dot_nt(qgb, kc) - slope2 * cdist.astype(F32)
        s = jnp.where(cmask, s, NEG)
        m = jnp.max(s, axis=1, keepdims=True)
        m = jnp.where(m > 0.5 * NEG, m, 0.0)
        e = jnp.where(cmask, jnp.exp2(s - m), 0.0)
        den = jnp.sum(e, axis=1, keepdims=True)
        p_cmp = e / jnp.where(den > 0.0, den, 1.0)
        o_cmp = _dot(p_cmp.astype(BF16), vc)
        imp = p_cmp[:tq] + p_cmp[tq:]
        cur = t_col // NSA_BLOCK
        forced = (blk == 0) | (blk == cur) | (blk == cur - 1)
        score = jnp.where(forced, NSA_FORCE_SCORE, imp)
        score = jnp.where(blk <= cur, score, NEG)
        sel = _top_select(score, NSA_TOPN, blk_f).astype(BF16)
        blk_col = lax.broadcasted_iota(jnp.int32, (nbp, 1), 0)

        def sel_body(j, carry, qgb=qgb, slope2=slope2, sel=sel, blk_col=blk_col):
            m, l, acc = carry
            sl = pl.ds(pl.multiple_of(j * TILE, TILE), TILE)
            kpos = j * TILE + key_i
            expand = (blk_col == (kpos // NSA_BLOCK)).astype(BF16)
            chosen = _dot(sel, expand)
            chosen = jnp.concatenate([chosen, chosen], axis=0)
            ok = (chosen > 0.5) & (kpos <= t2)
            s = _dot_nt(qgb, skb_ref[sl, :]) + slope2 * (kpos - t0).astype(F32)
            s = jnp.where(ok, s, NEG)
            m_new = jnp.maximum(m, jnp.max(s, axis=1, keepdims=True))
            alpha = jnp.exp2(m - m_new)
            p = jnp.where(ok, jnp.exp2(s - m_new), 0.0)
            l = alpha * l + jnp.sum(p, axis=1, keepdims=True)
            acc = alpha * acc + _dot(p.astype(BF16), svb_ref[sl, :])
            return m_new, l, acc

        init = (jnp.full((2 * tq, 1), NEG, F32), jnp.zeros((2 * tq, 1), F32), jnp.zeros((2 * tq, LANES), F32))
        m, l, acc = lax.fori_loop(0, n_sel_tiles, sel_body, init)
        o_sel = acc / jnp.where(l > 0.0, l, 1.0)

        def win_body(j, carry, qgb=qgb, slope2=slope2):
            m, l, acc = carry
            sl = pl.ds(pl.multiple_of(j * TILE, TILE), TILE)
            kpos = w_off + j * TILE + key_i
            wdist = t2 - kpos
            ok = (wdist >= 0) & (wdist < NSA_WINDOW)
            s = _dot_nt(qgb, wkb_ref[sl, :]) + slope2 * (kpos - t0).astype(F32)
            s = jnp.where(ok, s, NEG)
            m_new = jnp.maximum(m, jnp.max(s, axis=1, keepdims=True))
            alpha = jnp.exp2(m - m_new)
            p = jnp.where(ok, jnp.exp2(s - m_new), 0.0)
            l = alpha * l + jnp.sum(p, axis=1, keepdims=True)
            acc = alpha * acc + _dot(p.astype(BF16), wvb_ref[sl, :])
            return m_new, l, acc

        m, l, acc = lax.fori_loop(w_lo, w_hi, win_body, init)
        o_win = acc / jnp.where(l > 0.0, l, 1.0)
        res.append((o_cmp, o_sel, o_win))

    low = lane < HEAD_DIM
    out = jnp.zeros((tq, BRANCH), F32)
    for br in range(3):
        first = jnp.where(low, res[0][br][:tq], res[1][br][:tq])
        second = jnp.where(low, res[0][br][tq:], res[1][br][tq:])
        out = out + gates[br] * jnp.concatenate([first, second], axis=1)
    o_ref[...] = out


def _nsa_attend(q_arr, q_lane_blk, small, tok, selkv, win, egate, *, tq, q_off, w_off, prompt, slopes):
    b, tq_total, _ = small.shape
    sel_len, win_len, nbp = selkv.shape[1], win.shape[1], tok.shape[2]
    nt = tq_total // tq
    kern = functools.partial(_nsa_kernel, tq=tq, sel_len=sel_len, win_len=win_len, nbp=nbp, q_off=q_off, w_off=w_off,
                             prompt=prompt, slopes=slopes)
    return pl.pallas_call(
        kern,
        grid=(b, nt),
        in_specs=[
            pl.BlockSpec((None, tq, BRANCH), lambda bi, i: (bi, i, q_lane_blk)),
            pl.BlockSpec((None, tq, LANES), lambda bi, i: (bi, i, 0)),
            pl.BlockSpec((None, 2, nbp, LANES), lambda bi, i: (bi, 0, 0, 0)),
            pl.BlockSpec((None, sel_len, LANES), lambda bi, i: (bi, 0, 2)),
            pl.BlockSpec((None, sel_len, LANES), lambda bi, i: (bi, 0, 3)),
            pl.BlockSpec((None, win_len, LANES), lambda bi, i: (bi, 0, 0)),
            pl.BlockSpec((None, win_len, LANES), lambda bi, i: (bi, 0, 1)),
            pl.BlockSpec((3, LANES, BRANCH), lambda bi, i: (0, 0, 0)),
        ],
        out_specs=pl.BlockSpec((None, tq, BRANCH), lambda bi, i: (bi, i, 0)),
        out_shape=jax.ShapeDtypeStruct((b, tq_total, BRANCH), F32),
        scratch_shapes=[pltpu.VMEM((sel_len, LANES), BF16), pltpu.VMEM((sel_len, LANES), BF16),
                        pltpu.VMEM((win_len, LANES), BF16), pltpu.VMEM((win_len, LANES), BF16)],
        compiler_params=pltpu.CompilerParams(dimension_semantics=("arbitrary", "arbitrary"),
                                             vmem_limit_bytes=VMEM_LIMIT),
        name="nsa_prompt" if prompt else "nsa_sample",
    )(q_arr, small, tok, selkv, selkv, win, win, egate)


def _stack_queries(q, n_sub, sub_width, scale):
    lane = lax.broadcasted_iota(jnp.int32, (1, BRANCH), 1) // sub_width
    return jnp.concatenate([jnp.where(lane == s, q, 0.0) * scale for s in range(n_sub)], axis=0)


def _page_scores(qs, pages):
    return jnp.concatenate([_dot_nt(qs, pg[:, :BRANCH].astype(BF16)) for pg in pages], axis=1)


def _page_values(p, pages):
    page_rows = pages[0].shape[0]
    acc = None
    for n, pg in enumerate(pages):
        term = _dot(p[:, n * page_rows:(n + 1) * page_rows].astype(BF16), pg[:, BRANCH:].astype(BF16))
        acc = term if acc is None else acc + term
    return acc


def _pad_rows(x, rows):
    return jnp.concatenate([x, jnp.zeros((rows - x.shape[0], x.shape[1]), x.dtype)], axis=0)


def _stream_kernel(pt_ref, q_ref, new_ref, *rest, mode, n_rows, past, slopes, lam_init):
    pages = rest[:PAGES_PER_STEP]
    rest = rest[PAGES_PER_STEP:]
    if mode == "diff":
        lam_ref, sg_ref, o_ref, qs_ref, m_ref, l_ref, acc_ref = rest
        n_sub, sub_width = 2 * N_HEADS, DIFF_DIM
    else:
        suf_ref, cknew_ref, o_ref, qs_ref, m_ref, l_ref, acc_ref = rest
        n_sub, sub_width = N_HEADS, HEAD_DIM
    c = pl.program_id(1)
    page_rows = pages[0].shape[0]
    step_keys = PAGES_PER_STEP * page_rows
    rows = n_sub * n_rows
    sub_of_row = lax.broadcasted_iota(jnp.int32, (rows, 1), 0) // n_rows

    @pl.when(c == 0)
    def _():
        qs_ref[...] = _stack_queries(q_ref[...], n_sub, sub_width, LOG2E / math.sqrt(sub_width)).astype(BF16)
        m_ref[...] = jnp.full_like(m_ref, NEG)
        l_ref[...] = jnp.zeros_like(l_ref)
        acc_ref[...] = jnp.zeros_like(acc_ref)

    qs = qs_ref[...]
    if mode == "diff":
        slope_col = jnp.zeros((rows, 1), F32)
        for h in range(N_HEADS):
            slope_col = jnp.where(sub_of_row // 2 == h, float(slopes[h]) * LOG2E, slope_col)
    vals = [pg[...] for pg in pages]
    s = _page_scores(qs, vals)
    if mode == "diff":
        rel = (c * step_keys - past + lax.broadcasted_iota(jnp.int32, (1, step_keys), 1)).astype(F32)
        s = s + slope_col * rel
    else:
        suf = suf_ref[...] * LOG2E
        s = s + jnp.concatenate([jnp.broadcast_to(suf[h:h + 1, :], (n_rows, step_keys)) for h in range(N_HEADS)], axis=0)
    m, alpha, p, l = _online(s, m_ref[...], l_ref[...])
    m_ref[...] = m
    l_ref[...] = l
    acc_ref[...] = alpha * acc_ref[...] + _page_values(p, vals)

    @pl.when(c == pl.num_programs(1) - 1)
    def _():
        new = new_ref[...]
        kn = _pad_rows(new[:, :BRANCH], LANES).astype(BF16)
        vn = _pad_rows(new[:, BRANCH:], LANES).astype(BF16)
        col = lax.broadcasted_iota(jnp.int32, (1, LANES), 1)
        qrow = lax.broadcasted_iota(jnp.int32, (rows, 1), 0) % n_rows
        s = _dot_nt(qs, kn)
        if mode == "diff":
            s = s + slope_col * col.astype(F32)
        else:
            ckn = cknew_ref[...] * LOG2E
            s = s - jnp.concatenate([jnp.broadcast_to(ckn[h:h + 1, :], (n_rows, LANES)) for h in range(N_HEADS)], axis=0)
        s = jnp.where(col <= qrow, s, NEG)
        m, alpha, p, l = _online(s, m_ref[...], l_ref[...])
        o = (alpha * acc_ref[...] + _dot(p.astype(BF16), vn)) / l
        lane_head = lax.broadcasted_iota(jnp.int32, (1, BRANCH), 1) // HEAD_DIM
        out = jnp.zeros((n_rows, BRANCH), F32)
        if mode == "diff":
            lam = lam_ref[...]
            for h in range(N_HEADS):
                mask_h = lane_head == h
                a0 = o[(2 * h) * n_rows:(2 * h + 1) * n_rows]
                a1 = o[(2 * h + 1) * n_rows:(2 * h + 2) * n_rows]
                oh = jnp.where(mask_h, a0 - lam * a1, 0.0)
                ms = jnp.sum(oh * oh, axis=1, keepdims=True) * (1.0 / HEAD_DIM)
                oh = (oh * lax.rsqrt(ms + RMS_EPS)) * sg_ref[...] * (1.0 - lam_init)
                out = jnp.where(mask_h, oh, out)
        else:
            for h in range(N_HEADS):
                out = jnp.where(lane_head == h, o[h * n_rows:(h + 1) * n_rows], out)
        o_ref[...] = out


def _moba_sample_kernel(pt_ref, q_ref, new_ref, *rest, n_rows, past, slopes):
    pages = rest[:PAGES_PER_STEP]
    o_ref, qf_ref, gate_ref, mb_ref, lb_ref, accb_ref = rest[PAGES_PER_STEP:]
    c = pl.program_id(1)
    page_rows = pages[0].shape[0]
    pages_per_block = MOBA_BLOCK // page_rows
    blocks_per_step = PAGES_PER_STEP // pages_per_block
    rows = N_HEADS * n_rows
    head_of_row = lax.broadcasted_iota(jnp.int32, (rows, 1), 0) // n_rows
    lane = lax.broadcasted_iota(jnp.int32, (1, LANES), 1)
    slope_col = jnp.zeros((rows, 1), F32)
    for h in range(N_HEADS):
        slope_col = jnp.where(head_of_row == h, float(slopes[h]) * LOG2E, slope_col)

    @pl.when(c == 0)
    def _():
        qf_ref[...] = _stack_queries(q_ref[...], N_HEADS, HEAD_DIM, 1.0)
        gate_ref[...] = jnp.full_like(gate_ref, NEG)
        mb_ref[...] = jnp.full_like(mb_ref, NEG)
        lb_ref[...] = jnp.zeros_like(lb_ref)

    qf = qf_ref[...]
    qs = (qf * (LOG2E / math.sqrt(HEAD_DIM))).astype(BF16)
    key_f = lax.broadcasted_iota(jnp.int32, (1, MOBA_BLOCK), 1).astype(F32)
    for nb in range(blocks_per_step):
        vals = [pages[nb * pages_per_block + t][...] for t in range(pages_per_block)]
        n = c * blocks_per_step + nb
        ksum = None
        for pg in vals:
            part = jnp.sum(pg[:, :BRANCH], axis=0, keepdims=True)
            ksum = part if ksum is None else ksum + part
        kmean = ksum * (1.0 / MOBA_BLOCK)
        gate = jnp.sum(qf * kmean, axis=1, keepdims=True)
        gate_ref[...] = jnp.where(lane == n, gate, gate_ref[...])
        rel = (n * MOBA_BLOCK - past).astype(F32) + key_f
        s = _page_scores(qs, vals) + slope_col * rel
        m = jnp.max(s, axis=1, keepdims=True)
        p = jnp.exp2(s - m)
        mb_ref[...] = jnp.where(lane == n, m, mb_ref[...])
        lb_ref[...] = jnp.where(lane == n, jnp.sum(p, axis=1, keepdims=True), lb_ref[...])
        accb_ref[n] = _page_values(p, vals)

    @pl.when(c == pl.num_programs(1) - 1)
    def _():
        n_blocks = past // MOBA_BLOCK
        lane_f = lane.astype(F32)
        sel = _top_select(gate_ref[...], MOBA_TOPK, lane_f)
        new = new_ref[...]
        kn = _pad_rows(new[:, :BRANCH], LANES).astype(BF16)
        vn = _pad_rows(new[:, BRANCH:], LANES).astype(BF16)
        qrow = lax.broadcasted_iota(jnp.int32, (rows, 1), 0) % n_rows
        s = _dot_nt(qs, kn) + slope_col * lane_f
        s = jnp.where(lane <= qrow, s, NEG)
        m_own = jnp.max(s, axis=1, keepdims=True)
        mb = jnp.where(sel > 0.5, mb_ref[...], NEG)
        m_tot = jnp.maximum(m_own, jnp.max(mb, axis=1, keepdims=True))
        p = jnp.exp2(s - m_tot)
        wgt = jnp.where(sel > 0.5, jnp.exp2(mb - m_tot), 0.0)
        l = jnp.sum(p, axis=1, keepdims=True) + jnp.sum(wgt * lb_ref[...], axis=1, keepdims=True)
        acc = _dot(p.astype(BF16), vn)
        for n in range(n_blocks):
            acc = acc + wgt[:, n:n + 1] * accb_ref[n]
        o = acc / l
        lane_head = lax.broadcasted_iota(jnp.int32, (1, BRANCH), 1) // HEAD_DIM
        out = jnp.zeros((n_rows, BRANCH), F32)
        for h in range(N_HEADS):
            out = jnp.where(lane_head == h, o[h * n_rows:(h + 1) * n_rows], out)
        o_ref[...] = out


def _sample_call(kern, page_table, layer, q4s, lane_blk, new_kv, cache, extras, extra_specs, scratch, n_rows, name):
    b, n_pages = page_table.shape
    page_rows = cache.shape[2]
    nch = n_pages // PAGES_PER_STEP

    def page_spec(t):
        return pl.BlockSpec((None, None, page_rows, 2 * BRANCH),
                            lambda bi, c, pt: (layer, pt[bi, c * PAGES_PER_STEP + t], 0, 0))

    in_specs = [
        pl.BlockSpec((n_rows, BRANCH), lambda bi, c, pt: (bi, lane_blk)),
        pl.BlockSpec((n_rows, 2 * BRANCH), lambda bi, c, pt: (bi, 0)),
    ] + [page_spec(t) for t in range(PAGES_PER_STEP)] + extra_specs
    grid_spec = pltpu.PrefetchScalarGridSpec(
        num_scalar_prefetch=1,
        grid=(b, nch),
        in_specs=in_specs,
        out_specs=pl.BlockSpec((n_rows, BRANCH), lambda bi, c, pt: (bi, 0)),
        scratch_shapes=scratch,
    )
    return pl.pallas_call(
        kern,
        grid_spec=grid_spec,
        out_shape=jax.ShapeDtypeStruct((b * n_rows, BRANCH), F32),
        compiler_params=pltpu.CompilerParams(dimension_semantics=("arbitrary", "arbitrary"),
                                             vmem_limit_bytes=VMEM_LIMIT),
        name=name,
    )(page_table, q4s, new_kv, *([cache] * PAGES_PER_STEP), *extras)


def _suffix_kernel(x_ref, o_ref):
    n_pages, width = x_ref.shape
    x = x_ref[...]
    src = lax.broadcasted_iota(jnp.int32, (width, width), 0)
    dst = lax.broadcasted_iota(jnp.int32, (width, width), 1)
    same_head = (src % N_HEADS) == (dst // LANES)
    later = (src // N_HEADS) > (dst % LANES)
    within = _dot(x, (same_head & later).astype(F32), HIGHEST)
    total = _dot(x, same_head.astype(F32), HIGHEST)
    pr = lax.broadcasted_iota(jnp.int32, (n_pages, n_pages), 0)
    pc = lax.broadcasted_iota(jnp.int32, (n_pages, n_pages), 1)
    o_ref[...] = within + _dot((pc > pr).astype(F32), total, HIGHEST)


def _suffix_sums(logf_pages):
    b, n_pages, width = logf_pages.shape
    return pl.pallas_call(
        _suffix_kernel,
        grid=(b,),
        in_specs=[pl.BlockSpec((None, n_pages, width), lambda bi: (bi, 0, 0))],
        out_specs=pl.BlockSpec((None, n_pages, width), lambda bi: (bi, 0, 0)),
        out_shape=jax.ShapeDtypeStruct((b, n_pages, width), F32),
        compiler_params=pltpu.CompilerParams(dimension_semantics=("arbitrary",)),
        name="fox_suffix",
    )(logf_pages)


def _heads_to_rows(x, lo):
    y = jnp.swapaxes(x[:, :, lo:lo + N_HEADS], 1, 2)
    return jnp.pad(y, ((0, 0), (0, 8 - N_HEADS), (0, 0)))


def kernel(x_prompt, x_sample, cache_a_kv, cache_b_kv, cache_c_kv, cache_c_win, cache_d_kv, cache_d_logf, page_table,
           norm_g, w_in, w_out, diff_lambda, diff_subln_g, nsa_pe, nsa_phi_w1, nsa_phi_w2, fox_bias, final_g):
    depth = w_in.shape[0]
    bp, seq, _ = x_prompt.shape
    bs, dec, _ = x_sample.shape
    n_pool, page_rows = cache_a_kv.shape[1], cache_a_kv.shape[2]
    n_pages = page_table.shape[1]
    past = n_pages * page_rows
    win_c = cache_c_win.shape[2]
    slopes = _alibi_slopes()
    assert seq % TILE == 0 and (bs * dec) % TILE == 0 and dec == 8 and n_pages % PAGES_PER_STEP == 0
    assert past % MOBA_BLOCK == 0 and MOBA_BLOCK % page_rows == 0 and page_rows == LANES

    perm = _in_perm()
    w_in_p = jnp.pad(w_in[:, :, perm], ((0, 0), (0, 0), (0, N_IN_PAD - N_IN_ORIG))).astype(BF16)
    hp = np.array(C_HEAD_ORDER)
    crow = 2 * BRANCH + (hp[:, None] * HEAD_DIM + np.arange(HEAD_DIM)[None, :]).reshape(-1)
    out_rows = np.concatenate([np.arange(2 * BRANCH), crow, np.arange(3 * BRANCH, 4 * BRANCH)])
    w_out_p = w_out[:, out_rows, :].astype(BF16)
    fb = jnp.zeros((depth, 1, LANES), F32).at[:, 0, SMALL_F:SMALL_F + N_HEADS].set(fox_bias)
    norm_rows = norm_g.reshape(depth, 1, D_MODEL)
    final_row = final_g.reshape(1, D_MODEL)
    lam_v = diff_lambda.astype(F32)
    lam_dyn = jnp.exp(jnp.sum(lam_v[:, 0] * lam_v[:, 1], axis=-1)) - jnp.exp(jnp.sum(lam_v[:, 2] * lam_v[:, 3], axis=-1))
    subln = jnp.tile(diff_subln_g, (1, N_HEADS)).reshape(depth, 1, BRANCH)
    egate = np.zeros((3, LANES, BRANCH), np.float32)
    for br in range(3):
        for pos, h in enumerate(C_HEAD_ORDER):
            egate[br, h * 3 + br, pos * HEAD_DIM:(pos + 1) * HEAD_DIM] = 1.0
    egate = jnp.asarray(egate)
    pe2 = jnp.tile(nsa_pe, (1, 1, 1, NSA_GROUPS))
    w1 = nsa_phi_w1.reshape(depth, 2, NSA_BLOCK, HEAD_DIM, NSA_HIDDEN)
    z1 = jnp.zeros_like(w1)
    w1bd = jnp.concatenate([jnp.concatenate([w1, z1], axis=-1), jnp.concatenate([z1, w1], axis=-1)], axis=-2).astype(BF16)
    z2 = jnp.zeros_like(nsa_phi_w2)
    w2bd = jnp.concatenate([jnp.concatenate([nsa_phi_w2, z2], axis=-1),
                            jnp.concatenate([z2, nsa_phi_w2], axis=-1)], axis=-2).astype(BF16)

    cache_a = cache_a_kv.reshape(depth, n_pool, page_rows, 2 * BRANCH)
    cache_b = cache_b_kv.reshape(depth, n_pool, page_rows, 2 * BRANCH)
    cache_c = cache_c_kv.reshape(depth, n_pool, page_rows, 2 * BRANCH)
    cache_d = cache_d_kv.reshape(depth, n_pool, page_rows, 2 * BRANCH)
    cache_w = cache_c_win.reshape(depth, bs, win_c, BRANCH)
    cache_f = cache_d_logf.reshape(depth, n_pool, page_rows * N_HEADS)

    nb_p = seq // NSA_BLOCK
    nbp_p = -(-nb_p // LANES) * LANES
    sel_len_s = -(-(past + dec) // TILE) * TILE
    nb_s = sel_len_s // NSA_BLOCK
    nbp_s = -(-nb_s // LANES) * LANES
    win_len_s = -(-(win_c + dec) // TILE) * TILE

    xp = x_prompt.reshape(bp * seq, D_MODEL)
    xs = x_sample.reshape(bs * dec, D_MODEL)
    outs_p = [[] for _ in range(6)]
    outs_s = [[] for _ in range(6)]
    kv_scratch = lambda rows: [pltpu.VMEM((rows, 1), F32), pltpu.VMEM((rows, 1), F32), pltpu.VMEM((rows, BRANCH), F32)]
    const2 = lambda shape: pl.BlockSpec(shape, lambda bi, i: (0, 0))
    const3 = lambda shape: pl.BlockSpec(shape, lambda bi, c, pt: (0, 0))

    for l in range(depth):
        lam_init = 0.8 - 0.6 * math.exp(-0.3 * l)
        lam = (lam_dyn[l] + lam_init).reshape(1, 1)
        sg = subln[l]

        q4, g4, akv, bkv, ckv, cwin, dkv, small, logf, cum = _project(xp, norm_rows[l], w_in_p[l], fb[l], seq)
        r3 = lambda a: a.reshape(bp, seq, a.shape[-1])
        q4, akv, bkv, ckv, cwin, dkv, small3, cum3 = map(r3, (q4, akv, bkv, ckv, cwin, dkv, small, cum))
        o_a = _prompt_call(functools.partial(_moba_prompt_kernel, seq=seq, slopes=slopes[0]), q4, 0, akv,
                           [], [], [pltpu.VMEM((LANES, BRANCH), F32)], "moba_prompt")
        o_b = _prompt_call(functools.partial(_diff_prompt_kernel, seq=seq, slopes=slopes[1], lam_init=lam_init),
                           q4, 1, bkv, [lam, sg], [const2((1, 1)), const2((1, BRANCH))], [], "diff_prompt")
        tok = _compress(ckv, pe2[l], w1bd[l], w2bd[l], nb_p, nbp_p)
        o_c = _nsa_attend(q4, 2, small3, tok, ckv, cwin, egate, tq=TILE, q_off=0, w_off=0, prompt=True, slopes=slopes[2])
        ck_rows = _heads_to_rows(cum3, SMALL_F)
        o_d = _prompt_call(functools.partial(_fox_prompt_kernel, seq=seq), q4, 3, dkv, [ck_rows],
                           [pl.BlockSpec((None, 8, seq), lambda bi, i: (bi, 0, 0))], [], "fox_prompt")
        flat = lambda a: a.reshape(bp * seq, BRANCH)
        xp = _out_project(xp, [flat(o_a), flat(o_b), flat(o_c), flat(o_d)], g4, w_out_p[l], final_row, l == depth - 1)
        new_p = (akv, bkv, ckv, cwin[:, seq - min(NSA_WINDOW, seq):], dkv, logf.reshape(bp, seq, LANES)[:, :, SMALL_F:SMALL_F + N_HEADS])
        for k in range(6):
            outs_p[k].append(new_p[k])

        q4s, g4s, akv_s, bkv_s, ckv_s, cwin_s, dkv_s, small_s, logf_s, cum_s = _project(xs, norm_rows[l], w_in_p[l], fb[l], dec)
        o_a = _sample_call(functools.partial(_moba_sample_kernel, n_rows=dec, past=past, slopes=slopes[0]),
                           page_table, l, q4s, 0, akv_s, cache_a, [], [],
                           [pltpu.VMEM((N_HEADS * dec, BRANCH), F32), pltpu.VMEM((N_HEADS * dec, LANES), F32),
                            pltpu.VMEM((N_HEADS * dec, LANES), F32), pltpu.VMEM((N_HEADS * dec, LANES), F32),
                            pltpu.VMEM((past // MOBA_BLOCK, N_HEADS * dec, BRANCH), F32)], dec, "moba_sample")
        o_b = _sample_call(functools.partial(_stream_kernel, mode="diff", n_rows=dec, past=past, slopes=slopes[1], lam_init=lam_init),
                           page_table, l, q4s, 1, bkv_s, cache_b, [lam, sg], [const3((1, 1)), const3((1, BRANCH))],
                           [pltpu.VMEM((2 * N_HEADS * dec, BRANCH), BF16)] + kv_scratch(2 * N_HEADS * dec), dec, "diff_sample")
        suf = _suffix_sums(cache_f[l][page_table])
        suf = suf.reshape(bs, n_pages, N_HEADS, page_rows).transpose(0, 2, 1, 3).reshape(bs, N_HEADS, past)
        suf = jnp.pad(suf, ((0, 0), (0, 8 - N_HEADS), (0, 0)))
        ck_new = jnp.pad(_heads_to_rows(cum_s.reshape(bs, dec, LANES), SMALL_F), ((0, 0), (0, 0), (0, LANES - dec)))
        step_keys = PAGES_PER_STEP * page_rows
        o_d = _sample_call(functools.partial(_stream_kernel, mode="fox", n_rows=dec, past=past, slopes=None, lam_init=None),
                           page_table, l, q4s, 3, dkv_s, cache_d, [suf, ck_new],
                           [pl.BlockSpec((None, 8, step_keys), lambda bi, c, pt: (bi, 0, c)),
                            pl.BlockSpec((None, 8, LANES), lambda bi, c, pt: (bi, 0, 0))],
                           [pltpu.VMEM((N_HEADS * dec, BRANCH), BF16)] + kv_scratch(N_HEADS * dec), dec, "fox_sample")
        c_past = cache_c[l][page_table].reshape(bs, past, 2 * BRANCH)
        c_all = jnp.concatenate([c_past, ckv_s.reshape(bs, dec, 2 * BRANCH)], axis=1)
        c_all = jnp.pad(c_all, ((0, 0), (0, sel_len_s - past - dec), (0, 0)))
        w_all = jnp.concatenate([cache_w[l], cwin_s.reshape(bs, dec, BRANCH)], axis=1)
        w_pad = jnp.pad(w_all, ((0, 0), (0, win_len_s - win_c - dec), (0, 0)))
        tok_s = _compress(c_all, pe2[l], w1bd[l], w2bd[l], nb_s, nbp_s)
        o_c = _nsa_attend(q4s.reshape(bs, dec, 4 * BRANCH), 2, small_s.reshape(bs, dec, LANES), tok_s, c_all, w_pad, egate,
                          tq=dec, q_off=past, w_off=past - win_c, prompt=False, slopes=slopes[2])
        xs = _out_project(xs, [o_a, o_b, o_c.reshape(bs * dec, BRANCH), o_d], g4s, w_out_p[l], final_row, l == depth - 1)
        r3s = lambda a: a.reshape(bs, dec, a.shape[-1])
        new_s = (r3s(akv_s), r3s(bkv_s), r3s(ckv_s), w_all[:, dec:], r3s(dkv_s), r3s(logf_s)[:, :, SMALL_F:SMALL_F + N_HEADS])
        for k in range(6):
            outs_s[k].append(new_s[k])

    st = lambda parts, shape: jnp.stack(parts, axis=0).reshape((depth,) + shape)
    g = NSA_GROUPS
    y_prompt = xp.reshape(bp, seq, D_MODEL)
    y_sample = xs.reshape(bs, dec, D_MODEL)
    return (
        y_prompt, y_sample,
        st(outs_p[0], (bp, seq, 2, N_HEADS, HEAD_DIM)), st(outs_s[0], (bs, dec, 2, N_HEADS, HEAD_DIM)),
        st(outs_p[1], (bp, seq, 2, N_HEADS, HEAD_DIM)), st(outs_s[1], (bs, dec, 2, N_HEADS, HEAD_DIM)),
        st(outs_p[2], (bp, seq, 4, g, HEAD_DIM)), st(outs_s[2], (bs, dec, 4, g, HEAD_DIM)),
        st(outs_p[3], (bp, min(NSA_WINDOW, seq), 2, g, HEAD_DIM)), st(outs_s[3], (bs, win_c, 2, g, HEAD_DIM)),
        st(outs_p[4], (bp, seq, 2, N_HEADS, HEAD_DIM)), st(outs_s[4], (bs, dec, 2, N_HEADS, HEAD_DIM)),
        st(outs_p[5], (bp, seq, N_HEADS)), st(outs_s[5], (bs, dec, N_HEADS)),
    )
```

```python
import functools
import math

import numpy as np
import jax
import jax.numpy as jnp
from jax import lax
from jax.experimental import pallas as pl
from jax.experimental.pallas import tpu as pltpu

F32 = jnp.float32
BF16 = jnp.bfloat16
HIGHEST = lax.Precision.HIGHEST

D_MODEL = 1024
HEAD_DIM = 64
N_HEADS = 4
BRANCH = N_HEADS * HEAD_DIM
MOBA_BLOCK = 256
MOBA_TOPK = 3
DIFF_DIM = HEAD_DIM // 2
NSA_GROUPS = 2
NSA_BLOCK = 64
NSA_TOPN = 16
NSA_WINDOW = 512
NSA_HIDDEN = 128
NSA_FORCE_SCORE = 1.0e4
RMS_EPS = 1e-6
LOG2E = 1.4426950408889634
NEG = -1.0e30
LANES = 128
TILE = 256
PAGES_PER_STEP = 8
VMEM_LIMIT = 56 * 1024 * 1024

C_HEAD_ORDER = (0, 2, 1, 3)

N_IN_ORIG = 4368
COL_Q4 = 0
COL_G4 = 1024
COL_AKV = 2048
COL_BKV = 2560
COL_CKV = 3072
COL_CWIN = 3584
COL_DKV = 3840
COL_SMALL = 4352
N_IN_PAD = 4480
SMALL_F = 12


def _alibi_slopes():
    i = np.arange(1, 13, dtype=np.float32)
    s = np.exp2(np.float32(-8.0) * i / np.float32(12.0)).astype(np.float32)
    return s.reshape(N_HEADS, 3).T


def _in_perm():
    a = np.arange
    hp = np.array(C_HEAD_ORDER)
    cperm = (hp[:, None] * HEAD_DIM + a(HEAD_DIM)[None, :]).reshape(-1)
    cols = [
        a(0, 256), a(1024, 1280), 2048 + cperm, a(3340, 3596),
        a(768, 1024), a(1792, 2048), 3084 + cperm, a(4112, 4368),
        a(256, 768), a(1280, 1792), a(2304, 2816), a(2816, 3072), a(3596, 4108),
        a(3072, 3084), a(4108, 4112),
    ]
    return np.concatenate(cols)


def _dot_nt(a, b, precision=None):
    return lax.dot_general(a, b, (((1,), (1,)), ((), ())), preferred_element_type=F32, precision=precision)


def _dot(a, b, precision=None):
    return jnp.dot(a, b, preferred_element_type=F32, precision=precision)


def _top_select(score, k, lane_f):
    sel = jnp.zeros(score.shape, F32)
    cur = score
    for _ in range(k):
        mx = jnp.max(cur, axis=1, keepdims=True)
        idx = jnp.min(jnp.where(cur == mx, lane_f, 1.0e9), axis=1, keepdims=True)
        pick = lane_f == idx
        sel = jnp.where(pick & (mx > 0.5 * NEG), 1.0, sel)
        cur = jnp.where(pick, 3.0 * NEG, cur)
    return sel


def _online(s, m, l):
    m_new = jnp.maximum(m, jnp.max(s, axis=1, keepdims=True))
    alpha = jnp.exp2(m - m_new)
    p = jnp.exp2(s - m_new)
    return m_new, alpha, p, alpha * l + jnp.sum(p, axis=1, keepdims=True)


def _proj_kernel(x_ref, g_ref, w_ref, fb_ref, q4_ref, g4_ref, akv_ref, bkv_ref, ckv_ref, cwin_ref, dkv_ref,
                 small_ref, logf_ref, cum_ref, carry_ref, *, tm, seg_len):
    i = pl.program_id(0)
    x = x_ref[...]
    ms = jnp.mean(x * x, axis=-1, keepdims=True)
    h = ((x * lax.rsqrt(ms + RMS_EPS)) * g_ref[...]).astype(BF16)

    def mm(c0, c1):
        return _dot(h, w_ref[:, c0:c1])

    q4_ref[...] = mm(COL_Q4, COL_G4)
    g4_ref[...] = mm(COL_G4, COL_AKV)
    akv_ref[...] = mm(COL_AKV, COL_BKV)
    bkv_ref[...] = mm(COL_BKV, COL_CKV)
    ckv_ref[...] = mm(COL_CKV, COL_CWIN)
    cwin_ref[...] = mm(COL_CWIN, COL_DKV)
    dkv_ref[...] = mm(COL_DKV, COL_SMALL)
    small = mm(COL_SMALL, N_IN_PAD)
    small_ref[...] = small
    z = small + fb_ref[...]
    logf = jnp.minimum(z, 0.0) - jnp.log(1.0 + jnp.exp(-jnp.abs(z)))
    logf_ref[...] = logf
    r = lax.broadcasted_iota(jnp.int32, (tm, tm), 0)
    c = lax.broadcasted_iota(jnp.int32, (tm, tm), 1)
    if seg_len >= tm:
        tiles_per_seq = seg_len // tm

        @pl.when(i % tiles_per_seq == 0)
        def _():
            carry_ref[...] = jnp.zeros_like(carry_ref)

        tri = (c <= r).astype(F32)
        cum = _dot(tri, logf, HIGHEST) + carry_ref[...]
        cum_ref[...] = cum
        carry_ref[...] = cum[tm - 1:tm, :]
    else:
        tri = ((c <= r) & ((r // seg_len) == (c // seg_len))).astype(F32)
        cum_ref[...] = _dot(tri, logf, HIGHEST)


def _project(x2, g_row, w_l, fb_row, seg_len):
    m = x2.shape[0]
    tm = TILE
    widths = (1024, 1024, 512, 512, 512, 256, 512, LANES, LANES, LANES)
    row = lambda w: pl.BlockSpec((tm, w), lambda i: (i, 0))
    const = lambda shape: pl.BlockSpec(shape, lambda i: (0, 0))
    return pl.pallas_call(
        functools.partial(_proj_kernel, tm=tm, seg_len=seg_len),
        grid=(m // tm,),
        in_specs=[row(D_MODEL), const((1, D_MODEL)), const((D_MODEL, N_IN_PAD)), const((1, LANES))],
        out_specs=[row(w) for w in widths],
        out_shape=[jax.ShapeDtypeStruct((m, w), F32) for w in widths],
        scratch_shapes=[pltpu.VMEM((1, LANES), F32)],
        compiler_params=pltpu.CompilerParams(dimension_semantics=("arbitrary",), vmem_limit_bytes=VMEM_LIMIT),
        name="proj",
    )(x2, g_row, w_l, fb_row)


def _out_kernel(x_ref, oa_ref, ob_ref, oc_ref, od_ref, g4_ref, w_ref, fg_ref, y_ref, *, final):
    g4 = g4_ref[...]
    gate = g4 / (1.0 + jnp.exp(-g4))
    o4 = jnp.concatenate([oa_ref[...], ob_ref[...], oc_ref[...], od_ref[...]], axis=1)
    y = x_ref[...] + _dot((o4 * gate).astype(BF16), w_ref[...])
    if final:
        ms = jnp.mean(y * y, axis=-1, keepdims=True)
        y = (y * lax.rsqrt(ms + RMS_EPS)) * fg_ref[...]
    y_ref[...] = y


def _out_project(x2, outs, g4, w_l, fg_row, final):
    m = x2.shape[0]
    tm = TILE
    row = lambda w: pl.BlockSpec((tm, w), lambda i: (i, 0))
    const = lambda shape: pl.BlockSpec(shape, lambda i: (0, 0))
    return pl.pallas_call(
        functools.partial(_out_kernel, final=final),
        grid=(m // tm,),
        in_specs=[row(D_MODEL)] + [row(BRANCH)] * 4 + [row(D_MODEL), const((D_MODEL, D_MODEL)), const((1, D_MODEL))],
        out_specs=row(D_MODEL),
        out_shape=jax.ShapeDtypeStruct((m, D_MODEL), F32),
        compiler_params=pltpu.CompilerParams(dimension_semantics=("arbitrary",), vmem_limit_bytes=VMEM_LIMIT),
        name="outproj",
    )(x2, *outs, g4, w_l, fg_row)


def _cast_rows(src_ref, dst_ref, n_rows):
    def body(t, _):
        sl = pl.ds(pl.multiple_of(t * TILE, TILE), TILE)
        dst_ref[sl, :] = src_ref[sl, :].astype(BF16)
        return 0
    lax.fori_loop(0, n_rows // TILE, body, 0)


def _flash_multi(qs, kb_ref, vb_ref, acc_ref, i, own_bias, past_bias, row_bias=None):
    n_sub = len(qs)
    r = lax.broadcasted_iota(jnp.int32, (TILE, TILE), 0)
    c = lax.broadcasted_iota(jnp.int32, (TILE, TILE), 1)
    own = pl.ds(pl.multiple_of(i * TILE, TILE), TILE)
    k_own = kb_ref[own, :]
    v_own = vb_ref[own, :]
    stats = []
    for n in range(n_sub):
        s = _dot_nt(qs[n], k_own) + own_bias[n]
        s = jnp.where(c <= r, s, NEG)
        m = jnp.max(s, axis=1, keepdims=True)
        p = jnp.exp2(s - m)
        stats += [m, jnp.sum(p, axis=1, keepdims=True)]
        acc_ref[n] = _dot(p.astype(BF16), v_own)

    def body(j, carry):
        sl = pl.ds(pl.multiple_of(j * TILE, TILE), TILE)
        kt = kb_ref[sl, :]
        vt = vb_ref[sl, :]
        out = []
        for n in range(n_sub):
            s = _dot_nt(qs[n], kt) + past_bias[n](j)
            if row_bias is not None:
                s = s + row_bias[n](j)
            m, alpha, p, l = _online(s, carry[2 * n], carry[2 * n + 1])
            acc_ref[n] = alpha * acc_ref[n] + _dot(p.astype(BF16), vt)
            out += [m, l]
        return tuple(out)

    stats = lax.fori_loop(0, i, body, tuple(stats))
    return [acc_ref[n] / stats[2 * n + 1] for n in range(n_sub)]


def _moba_prompt_kernel(q_ref, k_ref, v_ref, o_ref, kb_ref, vb_ref, km_ref, acc_ref, *, seq, slopes):
    i = pl.program_id(1)
    nb = seq // MOBA_BLOCK

    @pl.when(i == 0)
    def _():
        _cast_rows(k_ref, kb_ref, seq)
        _cast_rows(v_ref, vb_ref, seq)
        km_ref[...] = jnp.zeros_like(km_ref)
        for n in range(nb):
            km_ref[n:n + 1, :] = jnp.mean(k_ref[n * MOBA_BLOCK:(n + 1) * MOBA_BLOCK, :], axis=0, keepdims=True)

    q = q_ref[...]
    lane_head = lax.broadcasted_iota(jnp.int32, (1, BRANCH), 1) // HEAD_DIM
    blk = lax.broadcasted_iota(jnp.int32, (1, LANES), 1)
    blk_f = blk.astype(F32)
    key_f = lax.broadcasted_iota(jnp.int32, (1, TILE), 1).astype(F32)
    scale = LOG2E / math.sqrt(HEAD_DIM)
    qs, own_bias, past_bias, row_bias = [], [], [], []
    for h in range(N_HEADS):
        qf = jnp.where(lane_head == h, q, 0.0)
        gate = _dot_nt(qf, km_ref[...], HIGHEST)
        gate = jnp.where(blk < i, gate, NEG)
        sel_bias = (1.0 - _top_select(gate, MOBA_TOPK, blk_f)) * NEG
        qs.append((qf * scale).astype(BF16))
        slope = float(slopes[h]) * LOG2E
        own_bias.append(slope * key_f)
        past_bias.append(lambda j, slope=slope: slope * ((j - i).astype(F32) * float(TILE) + key_f))
        row_bias.append(lambda j, sel_bias=sel_bias: jnp.sum(jnp.where(blk == j, sel_bias, 0.0), axis=1, keepdims=True))
    res = _flash_multi(qs, kb_ref, vb_ref, acc_ref, i, own_bias, past_bias, row_bias)
    out = res[0]
    for h in range(1, N_HEADS):
        out = jnp.where(lane_head == h, res[h], out)
    o_ref[...] = out


def _diff_prompt_kernel(q_ref, k_ref, v_ref, lam_ref, sg_ref, o_ref, kb_ref, vb_ref, acc_ref, *, seq, slopes, lam_init):
    i = pl.program_id(1)

    @pl.when(i == 0)
    def _():
        _cast_rows(k_ref, kb_ref, seq)
        _cast_rows(v_ref, vb_ref, seq)

    q = q_ref[...]
    lane = lax.broadcasted_iota(jnp.int32, (1, BRANCH), 1)
    key_f = lax.broadcasted_iota(jnp.int32, (1, TILE), 1).astype(F32)
    scale = LOG2E / math.sqrt(DIFF_DIM)
    lam = lam_ref[...]
    qs, own_bias, past_bias = [], [], []
    for sub in range(2 * N_HEADS):
        slope = float(slopes[sub // 2]) * LOG2E
        qs.append((jnp.where((lane // DIFF_DIM) == sub, q, 0.0) * scale).astype(BF16))
        own_bias.append(slope * key_f)
        past_bias.append(lambda j, slope=slope: slope * ((j - i).astype(F32) * float(TILE) + key_f))
    res = _flash_multi(qs, kb_ref, vb_ref, acc_ref, i, own_bias, past_bias)
    out = jnp.zeros((TILE, BRANCH), F32)
    for h in range(N_HEADS):
        mask_h = (lane // HEAD_DIM) == h
        o = jnp.where(mask_h, res[2 * h] - lam * res[2 * h + 1], 0.0)
        ms = jnp.sum(o * o, axis=1, keepdims=True) * (1.0 / HEAD_DIM)
        o = (o * lax.rsqrt(ms + RMS_EPS)) * sg_ref[...] * (1.0 - lam_init)
        out = jnp.where(mask_h, o, out)
    o_ref[...] = out


def _fox_prompt_kernel(q_ref, k_ref, v_ref, ck_ref, o_ref, kb_ref, vb_ref, acc_ref, *, seq):
    i = pl.program_id(1)

    @pl.when(i == 0)
    def _():
        _cast_rows(k_ref, kb_ref, seq)
        _cast_rows(v_ref, vb_ref, seq)

    q = q_ref[...]
    lane_head = lax.broadcasted_iota(jnp.int32, (1, BRANCH), 1) // HEAD_DIM
    scale = LOG2E / math.sqrt(HEAD_DIM)
    own = pl.ds(pl.multiple_of(i * TILE, TILE), TILE)
    qs, own_bias, past_bias = [], [], []
    for h in range(N_HEADS):
        qs.append((jnp.where(lane_head == h, q, 0.0) * scale).astype(BF16))
        ck_own = ck_ref[h:h + 1, own]
        c0 = ck_own[:, 0:1]
        own_bias.append((c0 - ck_own) * LOG2E)

        def bias(j, c0=c0, h=h):
            sl = pl.ds(pl.multiple_of(j * TILE, TILE), TILE)
            return (c0 - ck_ref[h:h + 1, sl]) * LOG2E

        past_bias.append(bias)
    res = _flash_multi(qs, kb_ref, vb_ref, acc_ref, i, own_bias, past_bias)
    out = res[0]
    for h in range(1, N_HEADS):
        out = jnp.where(lane_head == h, res[h], out)
    o_ref[...] = out


def _prompt_call(kern, q4, lane_blk, kv, extras, extra_specs, extra_scratch, name, n_sub=N_HEADS):
    b, seq, _ = kv.shape
    nt = seq // TILE
    in_specs = [
        pl.BlockSpec((None, TILE, BRANCH), lambda bi, i: (bi, i, lane_blk)),
        pl.BlockSpec((None, seq, BRANCH), lambda bi, i: (bi, 0, 0)),
        pl.BlockSpec((None, seq, BRANCH), lambda bi, i: (bi, 0, 1)),
    ] + extra_specs
    return pl.pallas_call(
        kern,
        grid=(b, nt),
        in_specs=in_specs,
        out_specs=pl.BlockSpec((None, TILE, BRANCH), lambda bi, i: (bi, i, 0)),
        out_shape=jax.ShapeDtypeStruct((b, seq, BRANCH), F32),
        scratch_shapes=[pltpu.VMEM((seq, BRANCH), BF16), pltpu.VMEM((seq, BRANCH), BF16)] + extra_scratch
        + [pltpu.VMEM((n_sub, TILE, BRANCH), F32)],
        compiler_params=pltpu.CompilerParams(dimension_semantics=("arbitrary", "arbitrary"),
                                             vmem_limit_bytes=VMEM_LIMIT),
        name=name,
    )(q4, kv, kv, *extras)


def _compress_kernel(x_ref, pe_ref, w1_ref, w2_ref, o_ref, *, nb, nbp):
    def body(s, acc):
        xs = x_ref[pl.ds(s, nb, stride=NSA_BLOCK), :] + pe_ref[pl.ds(s, 1), :]
        return acc + _dot(xs.astype(BF16), w1_ref[s])
    hid = lax.fori_loop(0, NSA_BLOCK, body, jnp.zeros((nb, NSA_GROUPS * NSA_HIDDEN), F32), unroll=8)
    hid = hid / (1.0 + jnp.exp(-hid))
    tok = _dot(hid.astype(BF16), w2_ref[...])
    if nbp > nb:
        tok = jnp.concatenate([tok, jnp.zeros((nbp - nb, LANES), F32)], axis=0)
    o_ref[...] = tok


def _compress(ckv, pe2, w1bd, w2bd, nb, nbp):
    b, length, _ = ckv.shape
    return pl.pallas_call(
        functools.partial(_compress_kernel, nb=nb, nbp=nbp),
        grid=(2, b),
        in_specs=[
            pl.BlockSpec((None, length, LANES), lambda c, bi: (bi, 0, c)),
            pl.BlockSpec((None, NSA_BLOCK, LANES), lambda c, bi: (c, 0, 0)),
            pl.BlockSpec((None, NSA_BLOCK, LANES, NSA_GROUPS * NSA_HIDDEN), lambda c, bi: (c, 0, 0, 0)),
            pl.BlockSpec((None, NSA_GROUPS * NSA_HIDDEN, LANES), lambda c, bi: (c, 0, 0)),
        ],
        out_specs=pl.BlockSpec((None, None, nbp, LANES), lambda c, bi: (bi, c, 0, 0)),
        out_shape=jax.ShapeDtypeStruct((b, 2, nbp, LANES), F32),
        compiler_params=pltpu.CompilerParams(dimension_semantics=("arbitrary", "arbitrary"),
                                             vmem_limit_bytes=VMEM_LIMIT),
        name="nsa_compress",
    )(ckv, pe2, w1bd, w2bd)


def _nsa_kernel(q_ref, sm_ref, tok_ref, sk_ref, sv_ref, wk_ref, wv_ref, eg_ref, o_ref,
                skb_ref, svb_ref, wkb_ref, wvb_ref, *, tq, tk, tkw, sel_len, win_len, nbp, q_off, w_off, prompt, slopes):
    i = pl.program_id(1)

    @pl.when(i == 0)
    def _():
        _cast_rows(sk_ref, skb_ref, sel_len)
        _cast_rows(sv_ref, svb_ref, sel_len)
        _cast_rows(wk_ref, wkb_ref, win_len)
        _cast_rows(wv_ref, wvb_ref, win_len)

    scale = LOG2E / math.sqrt(HEAD_DIM)
    q = q_ref[...] * scale
    lane = lax.broadcasted_iota(jnp.int32, (1, LANES), 1)
    t0 = q_off + i * tq
    t_col = t0 + lax.broadcasted_iota(jnp.int32, (tq, 1), 0)
    t2 = jnp.concatenate([t_col, t_col], axis=0)
    r2 = t2 - t0
    kc = tok_ref[0].astype(BF16)
    vc = tok_ref[1].astype(BF16)
    blk = lax.broadcasted_iota(jnp.int32, (1, nbp), 1)
    blk_f = blk.astype(F32)
    cend = blk * NSA_BLOCK + (NSA_BLOCK - 1)
    key_i = lax.broadcasted_iota(jnp.int32, (1, tk), 1)
    wkey_i = lax.broadcasted_iota(jnp.int32, (1, tkw), 1)
    sig = 1.0 / (1.0 + jnp.exp(-sm_ref[...]))
    gates = [_dot(sig, eg_ref[br], HIGHEST) for br in range(3)]
    n_sel_tiles = (i + 1) if prompt else sel_len // tk
    w_lo = jnp.maximum(i - 2, 0) if prompt else 0
    w_hi = (i + 1) if prompt else win_len // tkw

    blk_col = lax.broadcasted_iota(jnp.int32, (nbp, 1), 0)
    qgbs, slope2s, o_cmps, sels = [], [], [], []
    for g in range(NSA_GROUPS):
        mask_g = (lane // HEAD_DIM) == g
        qg = jnp.concatenate([jnp.where(mask_g, q[:, :LANES], 0.0), jnp.where(mask_g, q[:, LANES:], 0.0)], axis=0)
        qgb = qg.astype(BF16)
        h_a, h_b = C_HEAD_ORDER[g], C_HEAD_ORDER[2 + g]
        slope2 = jnp.concatenate([jnp.full((tq, 1), float(slopes[h_a]) * LOG2E, F32),
                                  jnp.full((tq, 1), float(slopes[h_b]) * LOG2E, F32)], axis=0)
        cdist = t2 - cend
        cmask = cdist >= 0
        s = _dot_nt(qgb, kc) - slope2 * cdist.astype(F32)
        s = jnp.where(cmask, s, NEG)
        m = jnp.max(s, axis=1, keepdims=True)
        m = jnp.where(m > 0.5 * NEG, m, 0.0)
        e = jnp.where(cmask, jnp.exp2(s - m), 0.0)
        den = jnp.sum(e, axis=1, keepdims=True)
        p_cmp = e / jnp.where(den > 0.0, den, 1.0)
        o_cmp = _dot(p_cmp.astype(BF16), vc)
        imp = p_cmp[:tq] + p_cmp[tq:]
        cur = t_col // NSA_BLOCK
        forced = (blk == 0) | (blk == cur) | (blk == cur - 1)
        score = jnp.where(forced, NSA_FORCE_SCORE, imp)
        score = jnp.where(blk <= cur, score, NEG)
        sels.append(_top_select(score, NSA_TOPN, blk_f).astype(BF16))
        qgbs.append(qgb)
        slope2s.append(slope2)
        o_cmps.append(o_cmp)

    def masked_update(s, ok, carry, v_tile):
        m, l, acc = carry
        s = jnp.where(ok, s, NEG)
        m_new = jnp.maximum(m, jnp.max(s, axis=1, keepdims=True))
        alpha = jnp.exp2(m - m_new)
        p = jnp.where(ok, jnp.exp2(s - m_new), 0.0)
        l = alpha * l + jnp.sum(p, axis=1, keepdims=True)
        return m_new, l, alpha * acc + _dot(p.astype(BF16), v_tile)

    def sel_body(j, carry):
        sl = pl.ds(pl.multiple_of(j * tk, tk), tk)
        kpos = j * tk + key_i
        expand = (blk_col == (kpos // NSA_BLOCK)).astype(BF16)
        k_tile, v_tile = skb_ref[sl, :], svb_ref[sl, :]
        rel = (kpos - t0).astype(F32)
        out = []
        for g in range(NSA_GROUPS):
            chosen = _dot(sels[g], expand)
            chosen = jnp.concatenate([chosen, chosen], axis=0)
            ok = (chosen > 0.5) & (kpos <= t2)
            s = _dot_nt(qgbs[g], k_tile) + slope2s[g] * rel
            out += list(masked_update(s, ok, carry[3 * g:3 * g + 3], v_tile))
        return tuple(out)

    def win_body(j, carry):
        sl = pl.ds(pl.multiple_of(j * tkw, tkw), tkw)
        kpos = w_off + j * tkw + wkey_i
        wdist = t2 - kpos
        ok = (wdist >= 0) & (wdist < NSA_WINDOW)
        k_tile, v_tile = wkb_ref[sl, :], wvb_ref[sl, :]
        rel = (kpos - t0).astype(F32)
        out = []
        for g in range(NSA_GROUPS):
            s = _dot_nt(qgbs[g], k_tile) + slope2s[g] * rel
            out += list(masked_update(s, ok, carry[3 * g:3 * g + 3], v_tile))
        return tuple(out)

    init = (jnp.full((2 * tq, 1), NEG, F32), jnp.zeros((2 * tq, 1), F32), jnp.zeros((2 * tq, LANES), F32)) * NSA_GROUPS
    fin = lax.fori_loop(0, n_sel_tiles, sel_body, init)
    o_sels = [fin[3 * g + 2] / jnp.where(fin[3 * g + 1] > 0.0, fin[3 * g + 1], 1.0) for g in range(NSA_GROUPS)]
    fin = lax.fori_loop(w_lo, w_hi, win_body, init)
    o_wins = [fin[3 * g + 2] / jnp.where(fin[3 * g + 1] > 0.0, fin[3 * g + 1], 1.0) for g in range(NSA_GROUPS)]
    res = [(o_cmps[g], o_sels[g], o_wins[g]) for g in range(NSA_GROUPS)]

    low = lane < HEAD_DIM
    out = jnp.zeros((tq, BRANCH), F32)
    for br in range(3):
        first = jnp.where(low, res[0][br][:tq], res[1][br][:tq])
        second = jnp.where(low, res[0][br][tq:], res[1][br][tq:])
        out = out + gates[br] * jnp.concatenate([first, second], axis=1)
    o_ref[...] = out


def _nsa_attend(q_arr, q_lane_blk, small, tok, selkv, win, egate, *, tq, tk, tkw, q_off, w_off, prompt, slopes):
    b, tq_total, _ = small.shape
    sel_len, win_len, nbp = selkv.shape[1], win.shape[1], tok.shape[2]
    nt = tq_total // tq
    assert sel_len % tk == 0 and win_len % tkw == 0 and (not prompt or tq == tk == tkw)
    kern = functools.partial(_nsa_kernel, tq=tq, tk=tk, tkw=tkw, sel_len=sel_len, win_len=win_len, nbp=nbp, q_off=q_off,
                             w_off=w_off, prompt=prompt, slopes=slopes)
    return pl.pallas_call(
        kern,
        grid=(b, nt),
        in_specs=[
            pl.BlockSpec((None, tq, BRANCH), lambda bi, i: (bi, i, q_lane_blk)),
            pl.BlockSpec((None, tq, LANES), lambda bi, i: (bi, i, 0)),
            pl.BlockSpec((None, 2, nbp, LANES), lambda bi, i: (bi, 0, 0, 0)),
            pl.BlockSpec((None, sel_len, LANES), lambda bi, i: (bi, 0, 2)),
            pl.BlockSpec((None, sel_len, LANES), lambda bi, i: (bi, 0, 3)),
            pl.BlockSpec((None, win_len, LANES), lambda bi, i: (bi, 0, 0)),
            pl.BlockSpec((None, win_len, LANES), lambda bi, i: (bi, 0, 1)),
            pl.BlockSpec((3, LANES, BRANCH), lambda bi, i: (0, 0, 0)),
        ],
        out_specs=pl.BlockSpec((None, tq, BRANCH), lambda bi, i: (bi, i, 0)),
        out_shape=jax.ShapeDtypeStruct((b, tq_total, BRANCH), F32),
        scratch_shapes=[pltpu.VMEM((sel_len, LANES), BF16), pltpu.VMEM((sel_len, LANES), BF16),
                        pltpu.VMEM((win_len, LANES), BF16), pltpu.VMEM((win_len, LANES), BF16)],
        compiler_params=pltpu.CompilerParams(dimension_semantics=("arbitrary", "arbitrary"),
                                             vmem_limit_bytes=VMEM_LIMIT),
        name="nsa_prompt" if prompt else "nsa_sample",
    )(q_arr, small, tok, selkv, selkv, win, win, egate)


def _stack_queries(q, n_sub, sub_width, scale):
    lane = lax.broadcasted_iota(jnp.int32, (1, BRANCH), 1) // sub_width
    return jnp.concatenate([jnp.where(lane == s, q, 0.0) * scale for s in range(n_sub)], axis=0)


def _page_scores(qs, pages):
    return jnp.concatenate([_dot(qs, pg[:BRANCH, :].astype(BF16)) for pg in pages], axis=1)


def _page_values(p, pages):
    page_rows = pages[0].shape[1]
    acc = None
    for n, pg in enumerate(pages):
        term = _dot_nt(p[:, n * page_rows:(n + 1) * page_rows].astype(BF16), pg[BRANCH:, :].astype(BF16))
        acc = term if acc is None else acc + term
    return acc


def _pad_rows(x, rows):
    return jnp.concatenate([x, jnp.zeros((rows - x.shape[0], x.shape[1]), x.dtype)], axis=0)


def _stream_kernel(pt_ref, q_ref, new_ref, *rest, mode, n_rows, past, slopes, lam_init):
    pages = rest[:PAGES_PER_STEP]
    rest = rest[PAGES_PER_STEP:]
    if mode == "diff":
        lam_ref, sg_ref, o_ref, qs_ref, m_ref, l_ref, acc_ref = rest
        n_sub, sub_width = 2 * N_HEADS, DIFF_DIM
    else:
        suf_ref, cknew_ref, o_ref, qs_ref, m_ref, l_ref, acc_ref = rest
        n_sub, sub_width = N_HEADS, HEAD_DIM
    c = pl.program_id(1)
    page_rows = pages[0].shape[1]
    step_keys = PAGES_PER_STEP * page_rows
    rows = n_sub * n_rows
    sub_of_row = lax.broadcasted_iota(jnp.int32, (rows, 1), 0) // n_rows

    @pl.when(c == 0)
    def _():
        qs_ref[...] = _stack_queries(q_ref[...], n_sub, sub_width, LOG2E / math.sqrt(sub_width)).astype(BF16)
        m_ref[...] = jnp.full_like(m_ref, NEG)
        l_ref[...] = jnp.zeros_like(l_ref)
        acc_ref[...] = jnp.zeros_like(acc_ref)

    qs = qs_ref[...]
    if mode == "diff":
        slope_col = jnp.zeros((rows, 1), F32)
        for h in range(N_HEADS):
            slope_col = jnp.where(sub_of_row // 2 == h, float(slopes[h]) * LOG2E, slope_col)
    vals = [pg[...] for pg in pages]
    s = _page_scores(qs, vals)
    if mode == "diff":
        rel = (c * step_keys - past + lax.broadcasted_iota(jnp.int32, (1, step_keys), 1)).astype(F32)
        s = s + slope_col * rel
    else:
        suf = suf_ref[...] * LOG2E
        s = s + jnp.concatenate([jnp.broadcast_to(suf[h:h + 1, :], (n_rows, step_keys)) for h in range(N_HEADS)], axis=0)
    m, alpha, p, l = _online(s, m_ref[...], l_ref[...])
    m_ref[...] = m
    l_ref[...] = l
    acc_ref[...] = alpha * acc_ref[...] + _page_values(p, vals)

    @pl.when(c == pl.num_programs(1) - 1)
    def _():
        new = new_ref[...]
        kn = _pad_rows(new[:, :BRANCH], LANES).astype(BF16)
        vn = _pad_rows(new[:, BRANCH:], LANES).astype(BF16)
        col = lax.broadcasted_iota(jnp.int32, (1, LANES), 1)
        qrow = lax.broadcasted_iota(jnp.int32, (rows, 1), 0) % n_rows
        s = _dot_nt(qs, kn)
        if mode == "diff":
            s = s + slope_col * col.astype(F32)
        else:
            ckn = cknew_ref[...] * LOG2E
            s = s - jnp.concatenate([jnp.broadcast_to(ckn[h:h + 1, :], (n_rows, LANES)) for h in range(N_HEADS)], axis=0)
        s = jnp.where(col <= qrow, s, NEG)
        m, alpha, p, l = _online(s, m_ref[...], l_ref[...])
        o = (alpha * acc_ref[...] + _dot(p.astype(BF16), vn)) / l
        lane_head = lax.broadcasted_iota(jnp.int32, (1, BRANCH), 1) // HEAD_DIM
        out = jnp.zeros((n_rows, BRANCH), F32)
        if mode == "diff":
            lam = lam_ref[...]
            for h in range(N_HEADS):
                mask_h = lane_head == h
                a0 = o[(2 * h) * n_rows:(2 * h + 1) * n_rows]
                a1 = o[(2 * h + 1) * n_rows:(2 * h + 2) * n_rows]
                oh = jnp.where(mask_h, a0 - lam * a1, 0.0)
                ms = jnp.sum(oh * oh, axis=1, keepdims=True) * (1.0 / HEAD_DIM)
                oh = (oh * lax.rsqrt(ms + RMS_EPS)) * sg_ref[...] * (1.0 - lam_init)
                out = jnp.where(mask_h, oh, out)
        else:
            for h in range(N_HEADS):
                out = jnp.where(lane_head == h, o[h * n_rows:(h + 1) * n_rows], out)
        o_ref[...] = out


def _moba_sample_kernel(pt_ref, q_ref, new_ref, *rest, n_rows, past, slopes):
    pages = rest[:PAGES_PER_STEP]
    o_ref, qf_ref, kmt_ref, mb_ref, lb_ref, accb_ref = rest[PAGES_PER_STEP:]
    c = pl.program_id(1)
    page_rows = pages[0].shape[1]
    pages_per_block = MOBA_BLOCK // page_rows
    blocks_per_step = PAGES_PER_STEP // pages_per_block
    rows = N_HEADS * n_rows
    head_of_row = lax.broadcasted_iota(jnp.int32, (rows, 1), 0) // n_rows
    lane = lax.broadcasted_iota(jnp.int32, (1, LANES), 1)
    slope_col = jnp.zeros((rows, 1), F32)
    for h in range(N_HEADS):
        slope_col = jnp.where(head_of_row == h, float(slopes[h]) * LOG2E, slope_col)

    @pl.when(c == 0)
    def _():
        qf_ref[...] = _stack_queries(q_ref[...], N_HEADS, HEAD_DIM, 1.0)
        kmt_ref[...] = jnp.zeros_like(kmt_ref)
        mb_ref[...] = jnp.full_like(mb_ref, NEG)
        lb_ref[...] = jnp.zeros_like(lb_ref)

    qf = qf_ref[...]
    qs = (qf * (LOG2E / math.sqrt(HEAD_DIM))).astype(BF16)
    key_f = lax.broadcasted_iota(jnp.int32, (1, MOBA_BLOCK), 1).astype(F32)
    for nb in range(blocks_per_step):
        vals = [pages[nb * pages_per_block + t][...] for t in range(pages_per_block)]
        n = c * blocks_per_step + nb
        ksum = None
        for pg in vals:
            part = jnp.sum(pg[:BRANCH, :], axis=1, keepdims=True)
            ksum = part if ksum is None else ksum + part
        kmt_ref[...] = jnp.where(lane == n, ksum * (1.0 / MOBA_BLOCK), kmt_ref[...])
        rel = (n * MOBA_BLOCK - past).astype(F32) + key_f
        s = _page_scores(qs, vals) + slope_col * rel
        m = jnp.max(s, axis=1, keepdims=True)
        p = jnp.exp2(s - m)
        mb_ref[...] = jnp.where(lane == n, m, mb_ref[...])
        lb_ref[...] = jnp.where(lane == n, jnp.sum(p, axis=1, keepdims=True), lb_ref[...])
        accb_ref[n] = _page_values(p, vals)

    @pl.when(c == pl.num_programs(1) - 1)
    def _():
        n_blocks = past // MOBA_BLOCK
        lane_f = lane.astype(F32)
        gate = _dot(qf, kmt_ref[...], HIGHEST)
        gate = jnp.where(lane < n_blocks, gate, NEG)
        sel = _top_select(gate, MOBA_TOPK, lane_f)
        new = new_ref[...]
        kn = _pad_rows(new[:, :BRANCH], LANES).astype(BF16)
        vn = _pad_rows(new[:, BRANCH:], LANES).astype(BF16)
        qrow = lax.broadcasted_iota(jnp.int32, (rows, 1), 0) % n_rows
        s = _dot_nt(qs, kn) + slope_col * lane_f
        s = jnp.where(lane <= qrow, s, NEG)
        m_own = jnp.max(s, axis=1, keepdims=True)
        mb = jnp.where(sel > 0.5, mb_ref[...], NEG)
        m_tot = jnp.maximum(m_own, jnp.max(mb, axis=1, keepdims=True))
        p = jnp.exp2(s - m_tot)
        wgt = jnp.where(sel > 0.5, jnp.exp2(mb - m_tot), 0.0)
        l = jnp.sum(p, axis=1, keepdims=True) + jnp.sum(wgt * lb_ref[...], axis=1, keepdims=True)
        acc = _dot(p.astype(BF16), vn)
        for n in range(n_blocks):
            acc = acc + wgt[:, n:n + 1] * accb_ref[n]
        o = acc / l
        lane_head = lax.broadcasted_iota(jnp.int32, (1, BRANCH), 1) // HEAD_DIM
        out = jnp.zeros((n_rows, BRANCH), F32)
        for h in range(N_HEADS):
            out = jnp.where(lane_head == h, o[h * n_rows:(h + 1) * n_rows], out)
        o_ref[...] = out


def _sample_call(kern, page_table, layer, q4s, lane_blk, new_kv, cache, extras, extra_specs, scratch, n_rows, name):
    b, n_pages = page_table.shape
    page_rows = cache.shape[3]
    nch = n_pages // PAGES_PER_STEP

    def page_spec(t):
        return pl.BlockSpec((None, None, 2 * BRANCH, page_rows),
                            lambda bi, c, pt: (layer, pt[bi, c * PAGES_PER_STEP + t], 0, 0))

    in_specs = [
        pl.BlockSpec((n_rows, BRANCH), lambda bi, c, pt: (bi, lane_blk)),
        pl.BlockSpec((n_rows, 2 * BRANCH), lambda bi, c, pt: (bi, 0)),
    ] + [page_spec(t) for t in range(PAGES_PER_STEP)] + extra_specs
    grid_spec = pltpu.PrefetchScalarGridSpec(
        num_scalar_prefetch=1,
        grid=(b, nch),
        in_specs=in_specs,
        out_specs=pl.BlockSpec((n_rows, BRANCH), lambda bi, c, pt: (bi, 0)),
        scratch_shapes=scratch,
    )
    return pl.pallas_call(
        kern,
        grid_spec=grid_spec,
        out_shape=jax.ShapeDtypeStruct((b * n_rows, BRANCH), F32),
        compiler_params=pltpu.CompilerParams(dimension_semantics=("arbitrary", "arbitrary"),
                                             vmem_limit_bytes=VMEM_LIMIT),
        name=name,
    )(page_table, q4s, new_kv, *([cache] * PAGES_PER_STEP), *extras)


def _suffix_kernel(x_ref, o_ref):
    n_pages, width = x_ref.shape
    x = x_ref[...]
    src = lax.broadcasted_iota(jnp.int32, (width, width), 0)
    dst = lax.broadcasted_iota(jnp.int32, (width, width), 1)
    same_head = (src // LANES) == (dst // LANES)
    later = (src % LANES) > (dst % LANES)
    within = _dot(x, (same_head & later).astype(F32), HIGHEST)
    total = _dot(x, same_head.astype(F32), HIGHEST)
    pr = lax.broadcasted_iota(jnp.int32, (n_pages, n_pages), 0)
    pc = lax.broadcasted_iota(jnp.int32, (n_pages, n_pages), 1)
    o_ref[...] = within + _dot((pc > pr).astype(F32), total, HIGHEST)


def _suffix_sums(logf_pages):
    b, n_pages, width = logf_pages.shape
    return pl.pallas_call(
        _suffix_kernel,
        grid=(b,),
        in_specs=[pl.BlockSpec((None, n_pages, width), lambda bi: (bi, 0, 0))],
        out_specs=pl.BlockSpec((None, n_pages, width), lambda bi: (bi, 0, 0)),
        out_shape=jax.ShapeDtypeStruct((b, n_pages, width), F32),
        compiler_params=pltpu.CompilerParams(dimension_semantics=("arbitrary",)),
        name="fox_suffix",
    )(logf_pages)


def _rows_kernel(pt_ref, new_ref, *rest, n_rows):
    pages = rest[:PAGES_PER_STEP]
    o_ref = rest[PAGES_PER_STEP]
    c = pl.program_id(1)
    page_rows = pages[0].shape[1]

    @pl.when(c < pl.num_programs(1) - 1)
    def _():
        for t, pg in enumerate(pages):
            o_ref[t * page_rows:(t + 1) * page_rows, :] = pg[...].T

    @pl.when(c == pl.num_programs(1) - 1)
    def _():
        o_ref[...] = jnp.zeros_like(o_ref)
        o_ref[0:n_rows, :] = new_ref[...]


def _sequence_rows(page_table, layer, new_rows, cache, n_rows):
    b, n_pages = page_table.shape
    page_rows = cache.shape[3]
    nch = n_pages // PAGES_PER_STEP
    chunk = PAGES_PER_STEP * page_rows

    def page_spec(t):
        return pl.BlockSpec((None, None, 2 * BRANCH, page_rows),
                            lambda bi, c, pt: (layer, pt[bi, jnp.minimum(c, nch - 1) * PAGES_PER_STEP + t], 0, 0))

    grid_spec = pltpu.PrefetchScalarGridSpec(
        num_scalar_prefetch=1,
        grid=(b, nch + 1),
        in_specs=[pl.BlockSpec((n_rows, 2 * BRANCH), lambda bi, c, pt: (bi, 0))] + [page_spec(t) for t in range(PAGES_PER_STEP)],
        out_specs=pl.BlockSpec((None, chunk, 2 * BRANCH), lambda bi, c, pt: (bi, c, 0)),
    )
    return pl.pallas_call(
        functools.partial(_rows_kernel, n_rows=n_rows),
        grid_spec=grid_spec,
        out_shape=jax.ShapeDtypeStruct((b, (nch + 1) * chunk, 2 * BRANCH), F32),
        compiler_params=pltpu.CompilerParams(dimension_semantics=("arbitrary", "arbitrary"),
                                             vmem_limit_bytes=VMEM_LIMIT),
        name="nsa_rows",
    )(page_table, new_rows, *([cache] * PAGES_PER_STEP))


def _heads_to_rows(x, lo):
    y = jnp.swapaxes(x[:, :, lo:lo + N_HEADS], 1, 2)
    return jnp.pad(y, ((0, 0), (0, 8 - N_HEADS), (0, 0)))


def kernel(x_prompt, x_sample, cache_a_kv, cache_b_kv, cache_c_kv, cache_c_win, cache_d_kv, cache_d_logf, page_table,
           norm_g, w_in, w_out, diff_lambda, diff_subln_g, nsa_pe, nsa_phi_w1, nsa_phi_w2, fox_bias, final_g):
    depth = w_in.shape[0]
    bp, seq, _ = x_prompt.shape
    bs, dec, _ = x_sample.shape
    n_pool, page_rows = cache_a_kv.shape[1], cache_a_kv.shape[2]
    n_pages = page_table.shape[1]
    past = n_pages * page_rows
    win_c = cache_c_win.shape[2]
    slopes = _alibi_slopes()
    assert seq % TILE == 0 and (bs * dec) % TILE == 0 and dec == 8 and n_pages % PAGES_PER_STEP == 0
    assert past % MOBA_BLOCK == 0 and MOBA_BLOCK % page_rows == 0 and page_rows == LANES

    perm = _in_perm()
    w_in_p = jnp.pad(w_in[:, :, perm], ((0, 0), (0, 0), (0, N_IN_PAD - N_IN_ORIG))).astype(BF16)
    hp = np.array(C_HEAD_ORDER)
    crow = 2 * BRANCH + (hp[:, None] * HEAD_DIM + np.arange(HEAD_DIM)[None, :]).reshape(-1)
    out_rows = np.concatenate([np.arange(2 * BRANCH), crow, np.arange(3 * BRANCH, 4 * BRANCH)])
    w_out_p = w_out[:, out_rows, :].astype(BF16)
    fb = jnp.zeros((depth, 1, LANES), F32).at[:, 0, SMALL_F:SMALL_F + N_HEADS].set(fox_bias)
    norm_rows = norm_g.reshape(depth, 1, D_MODEL)
    final_row = final_g.reshape(1, D_MODEL)
    lam_v = diff_lambda.astype(F32)
    lam_dyn = jnp.exp(jnp.sum(lam_v[:, 0] * lam_v[:, 1], axis=-1)) - jnp.exp(jnp.sum(lam_v[:, 2] * lam_v[:, 3], axis=-1))
    subln = jnp.tile(diff_subln_g, (1, N_HEADS)).reshape(depth, 1, BRANCH)
    egate = np.zeros((3, LANES, BRANCH), np.float32)
    for br in range(3):
        for pos, h in enumerate(C_HEAD_ORDER):
            egate[br, h * 3 + br, pos * HEAD_DIM:(pos + 1) * HEAD_DIM] = 1.0
    egate = jnp.asarray(egate)
    pe2 = jnp.tile(nsa_pe, (1, 1, 1, NSA_GROUPS))
    w1 = nsa_phi_w1.reshape(depth, 2, NSA_BLOCK, HEAD_DIM, NSA_HIDDEN)
    z1 = jnp.zeros_like(w1)
    w1bd = jnp.concatenate([jnp.concatenate([w1, z1], axis=-1), jnp.concatenate([z1, w1], axis=-1)], axis=-2).astype(BF16)
    z2 = jnp.zeros_like(nsa_phi_w2)
    w2bd = jnp.concatenate([jnp.concatenate([nsa_phi_w2, z2], axis=-1),
                            jnp.concatenate([z2, nsa_phi_w2], axis=-1)], axis=-2).astype(BF16)

    pages_fm = lambda c: jnp.transpose(c, (0, 1, 3, 4, 5, 2)).reshape(depth, n_pool, 2 * BRANCH, page_rows)
    cache_a, cache_b, cache_c, cache_d = map(pages_fm, (cache_a_kv, cache_b_kv, cache_c_kv, cache_d_kv))
    cache_w = cache_c_win.reshape(depth, bs, win_c, BRANCH)
    cache_f = jnp.swapaxes(cache_d_logf, 2, 3).reshape(depth, n_pool, N_HEADS * page_rows)

    nb_p = seq // NSA_BLOCK
    nbp_p = -(-nb_p // LANES) * LANES
    step_keys = PAGES_PER_STEP * page_rows
    sel_len_s = past + step_keys
    sel_tile_s = sel_len_s // 4
    assert sel_tile_s % LANES == 0 and sel_len_s % TILE == 0
    nb_s = sel_len_s // NSA_BLOCK
    nbp_s = -(-nb_s // LANES) * LANES
    win_len_s = -(-(win_c + dec) // TILE) * TILE

    xp = x_prompt.reshape(bp * seq, D_MODEL)
    xs = x_sample.reshape(bs * dec, D_MODEL)
    outs_p = [[] for _ in range(6)]
    outs_s = [[] for _ in range(6)]
    kv_scratch = lambda rows: [pltpu.VMEM((rows, 1), F32), pltpu.VMEM((rows, 1), F32), pltpu.VMEM((rows, BRANCH), F32)]
    const2 = lambda shape: pl.BlockSpec(shape, lambda bi, i: (0, 0))
    const3 = lambda shape: pl.BlockSpec(shape, lambda bi, c, pt: (0, 0))

    for l in range(depth):
        lam_init = 0.8 - 0.6 * math.exp(-0.3 * l)
        lam = (lam_dyn[l] + lam_init).reshape(1, 1)
        sg = subln[l]

        q4, g4, akv, bkv, ckv, cwin, dkv, small, logf, cum = _project(xp, norm_rows[l], w_in_p[l], fb[l], seq)
        r3 = lambda a: a.reshape(bp, seq, a.shape[-1])
        q4, akv, bkv, ckv, cwin, dkv, small3, cum3 = map(r3, (q4, akv, bkv, ckv, cwin, dkv, small, cum))
        o_a = _prompt_call(functools.partial(_moba_prompt_kernel, seq=seq, slopes=slopes[0]), q4, 0, akv,
                           [], [], [pltpu.VMEM((LANES, BRANCH), F32)], "moba_prompt")
        o_b = _prompt_call(functools.partial(_diff_prompt_kernel, seq=seq, slopes=slopes[1], lam_init=lam_init),
                           q4, 1, bkv, [lam, sg], [const2((1, 1)), const2((1, BRANCH))], [], "diff_prompt",
                           n_sub=2 * N_HEADS)
        tok = _compress(ckv, pe2[l], w1bd[l], w2bd[l], nb_p, nbp_p)
        o_c = _nsa_attend(q4, 2, small3, tok, ckv, cwin, egate, tq=TILE, tk=TILE, tkw=TILE, q_off=0, w_off=0, prompt=True,
                          slopes=slopes[2])
        ck_rows = _heads_to_rows(cum3, SMALL_F)
        o_d = _prompt_call(functools.partial(_fox_prompt_kernel, seq=seq), q4, 3, dkv, [ck_rows],
                           [pl.BlockSpec((None, 8, seq), lambda bi, i: (bi, 0, 0))], [], "fox_prompt")
        flat = lambda a: a.reshape(bp * seq, BRANCH)
        xp = _out_project(xp, [flat(o_a), flat(o_b), flat(o_c), flat(o_d)], g4, w_out_p[l], final_row, l == depth - 1)
        new_p = (akv, bkv, ckv, cwin[:, seq - min(NSA_WINDOW, seq):], dkv, logf.reshape(bp, seq, LANES)[:, :, SMALL_F:SMALL_F + N_HEADS])
        for k in range(6):
            outs_p[k].append(new_p[k])

        q4s, g4s, akv_s, bkv_s, ckv_s, cwin_s, dkv_s, small_s, logf_s, cum_s = _project(xs, norm_rows[l], w_in_p[l], fb[l], dec)
        o_a = _sample_call(functools.partial(_moba_sample_kernel, n_rows=dec, past=past, slopes=slopes[0]),
                           page_table, l, q4s, 0, akv_s, cache_a, [], [],
                           [pltpu.VMEM((N_HEADS * dec, BRANCH), F32), pltpu.VMEM((BRANCH, LANES), F32),
                            pltpu.VMEM((N_HEADS * dec, LANES), F32), pltpu.VMEM((N_HEADS * dec, LANES), F32),
                            pltpu.VMEM((past // MOBA_BLOCK, N_HEADS * dec, BRANCH), F32)], dec, "moba_sample")
        o_b = _sample_call(functools.partial(_stream_kernel, mode="diff", n_rows=dec, past=past, slopes=slopes[1], lam_init=lam_init),
                           page_table, l, q4s, 1, bkv_s, cache_b, [lam, sg], [const3((1, 1)), const3((1, BRANCH))],
                           [pltpu.VMEM((2 * N_HEADS * dec, BRANCH), BF16)] + kv_scratch(2 * N_HEADS * dec), dec, "diff_sample")
        suf = _suffix_sums(cache_f[l][page_table])
        suf = suf.reshape(bs, n_pages, N_HEADS, page_rows).transpose(0, 2, 1, 3).reshape(bs, N_HEADS, past)
        suf = jnp.pad(suf, ((0, 0), (0, 8 - N_HEADS), (0, 0)))
        ck_new = jnp.pad(_heads_to_rows(cum_s.reshape(bs, dec, LANES), SMALL_F), ((0, 0), (0, 0), (0, LANES - dec)))
        o_d = _sample_call(functools.partial(_stream_kernel, mode="fox", n_rows=dec, past=past, slopes=None, lam_init=None),
                           page_table, l, q4s, 3, dkv_s, cache_d, [suf, ck_new],
                           [pl.BlockSpec((None, 8, step_keys), lambda bi, c, pt: (bi, 0, c)),
                            pl.BlockSpec((None, 8, LANES), lambda bi, c, pt: (bi, 0, 0))],
                           [pltpu.VMEM((N_HEADS * dec, BRANCH), BF16)] + kv_scratch(N_HEADS * dec), dec, "fox_sample")
        c_all = _sequence_rows(page_table, l, ckv_s, cache_c, dec)
        w_all = jnp.concatenate([cache_w[l], cwin_s.reshape(bs, dec, BRANCH)], axis=1)
        w_pad = jnp.pad(w_all, ((0, 0), (0, win_len_s - win_c - dec), (0, 0)))
        tok_s = _compress(c_all, pe2[l], w1bd[l], w2bd[l], nb_s, nbp_s)
        o_c = _nsa_attend(q4s.reshape(bs, dec, 4 * BRANCH), 2, small_s.reshape(bs, dec, LANES), tok_s, c_all, w_pad, egate,
                          tq=dec, tk=sel_tile_s, tkw=win_len_s, q_off=past, w_off=past - win_c, prompt=False, slopes=slopes[2])
        xs = _out_project(xs, [o_a, o_b, o_c.reshape(bs * dec, BRANCH), o_d], g4s, w_out_p[l], final_row, l == depth - 1)
        r3s = lambda a: a.reshape(bs, dec, a.shape[-1])
        new_s = (r3s(akv_s), r3s(bkv_s), r3s(ckv_s), w_all[:, dec:], r3s(dkv_s), r3s(logf_s)[:, :, SMALL_F:SMALL_F + N_HEADS])
        for k in range(6):
            outs_s[k].append(new_s[k])

    st = lambda parts, shape: jnp.stack(parts, axis=0).reshape((depth,) + shape)
    g = NSA_GROUPS
    y_prompt = xp.reshape(bp, seq, D_MODEL)
    y_sample = xs.reshape(bs, dec, D_MODEL)
    return (
        y_prompt, y_sample,
        st(outs_p[0], (bp, seq, 2, N_HEADS, HEAD_DIM)), st(outs_s[0], (bs, dec, 2, N_HEADS, HEAD_DIM)),
        st(outs_p[1], (bp, seq, 2, N_HEADS, HEAD_DIM)), st(outs_s[1], (bs, dec, 2, N_HEADS, HEAD_DIM)),
        st(outs_p[2], (bp, seq, 4, g, HEAD_DIM)), st(outs_s[2], (bs, dec, 4, g, HEAD_DIM)),
        st(outs_p[3], (bp, min(NSA_WINDOW, seq), 2, g, HEAD_DIM)), st(outs_s[3], (bs, win_c, 2, g, HEAD_DIM)),
        st(outs_p[4], (bp, seq, 2, N_HEADS, HEAD_DIM)), st(outs_s[4], (bs, dec, 2, N_HEADS, HEAD_DIM)),
        st(outs_p[5], (bp, seq, N_HEADS)), st(outs_s[5], (bs, dec, N_HEADS)),
    )
```

```python
import functools
import math

import numpy as np
import jax
import jax.numpy as jnp
from jax import lax
from jax.experimental import pallas as pl
from jax.experimental.pallas import tpu as pltpu

F32 = jnp.float32
BF16 = jnp.bfloat16
HIGHEST = lax.Precision.HIGHEST

D_MODEL = 1024
HEAD_DIM = 64
N_HEADS = 4
BRANCH = N_HEADS * HEAD_DIM
MOBA_BLOCK = 256
MOBA_TOPK = 3
DIFF_DIM = HEAD_DIM // 2
NSA_GROUPS = 2
NSA_BLOCK = 64
NSA_TOPN = 16
NSA_WINDOW = 512
NSA_HIDDEN = 128
NSA_FORCE_SCORE = 1.0e4
RMS_EPS = 1e-6
LOG2E = 1.4426950408889634
NEG = -1.0e30
LANES = 128
TILE = 256
PAGES_PER_STEP = 16
VMEM_LIMIT = 56 * 1024 * 1024

C_HEAD_ORDER = (0, 2, 1, 3)

N_IN_ORIG = 4368
COL_Q4 = 0
COL_G4 = 1024
COL_AKV = 2048
COL_BKV = 2560
COL_CKV = 3072
COL_CWIN = 3584
COL_DKV = 3840
COL_SMALL = 4352
N_IN_PAD = 4480
SMALL_F = 12


def _alibi_slopes():
    i = np.arange(1, 13, dtype=np.float32)
    s = np.exp2(np.float32(-8.0) * i / np.float32(12.0)).astype(np.float32)
    return s.reshape(N_HEADS, 3).T


def _in_perm():
    a = np.arange
    hp = np.array(C_HEAD_ORDER)
    cperm = (hp[:, None] * HEAD_DIM + a(HEAD_DIM)[None, :]).reshape(-1)
    cols = [
        a(0, 256), a(1024, 1280), 2048 + cperm, a(3340, 3596),
        a(768, 1024), a(1792, 2048), 3084 + cperm, a(4112, 4368),
        a(256, 768), a(1280, 1792), a(2304, 2816), a(2816, 3072), a(3596, 4108),
        a(3072, 3084), a(4108, 4112),
    ]
    return np.concatenate(cols)


def _dot_nt(a, b, precision=None):
    return lax.dot_general(a, b, (((1,), (1,)), ((), ())), preferred_element_type=F32, precision=precision)


def _dot(a, b, precision=None):
    return jnp.dot(a, b, preferred_element_type=F32, precision=precision)


def _top_select(score, k, lane_f):
    sel = jnp.zeros(score.shape, F32)
    cur = score
    for _ in range(k):
        mx = jnp.max(cur, axis=1, keepdims=True)
        idx = jnp.min(jnp.where(cur == mx, lane_f, 1.0e9), axis=1, keepdims=True)
        pick = lane_f == idx
        sel = jnp.where(pick & (mx > 0.5 * NEG), 1.0, sel)
        cur = jnp.where(pick, 3.0 * NEG, cur)
    return sel


def _online(s, m, l):
    m_new = jnp.maximum(m, jnp.max(s, axis=1, keepdims=True))
    alpha = jnp.exp2(m - m_new)
    p = jnp.exp2(s - m_new)
    return m_new, alpha, p, alpha * l + jnp.sum(p, axis=1, keepdims=True)


def _proj_kernel(x_ref, g_ref, w_ref, fb_ref, q4_ref, g4_ref, akv_ref, bkv_ref, ckv_ref, cwin_ref, dkv_ref,
                 small_ref, logf_ref, cum_ref, carry_ref, *, tm, seg_len):
    i = pl.program_id(0)
    x = x_ref[...]
    ms = jnp.mean(x * x, axis=-1, keepdims=True)
    h = ((x * lax.rsqrt(ms + RMS_EPS)) * g_ref[...]).astype(BF16)

    def mm(c0, c1):
        return _dot(h, w_ref[:, c0:c1])

    q4_ref[...] = mm(COL_Q4, COL_G4)
    g4_ref[...] = mm(COL_G4, COL_AKV)
    akv_ref[...] = mm(COL_AKV, COL_BKV)
    bkv_ref[...] = mm(COL_BKV, COL_CKV)
    ckv_ref[...] = mm(COL_CKV, COL_CWIN)
    cwin_ref[...] = mm(COL_CWIN, COL_DKV)
    dkv_ref[...] = mm(COL_DKV, COL_SMALL)
    small = mm(COL_SMALL, N_IN_PAD)
    small_ref[...] = small
    z = small + fb_ref[...]
    logf = jnp.minimum(z, 0.0) - jnp.log(1.0 + jnp.exp(-jnp.abs(z)))
    logf_ref[...] = logf
    r = lax.broadcasted_iota(jnp.int32, (tm, tm), 0)
    c = lax.broadcasted_iota(jnp.int32, (tm, tm), 1)
    if seg_len >= tm:
        tiles_per_seq = seg_len // tm

        @pl.when(i % tiles_per_seq == 0)
        def _():
            carry_ref[...] = jnp.zeros_like(carry_ref)

        tri = (c <= r).astype(F32)
        cum = _dot(tri, logf, HIGHEST) + carry_ref[...]
        cum_ref[...] = cum
        carry_ref[...] = cum[tm - 1:tm, :]
    else:
        tri = ((c <= r) & ((r // seg_len) == (c // seg_len))).astype(F32)
        cum_ref[...] = _dot(tri, logf, HIGHEST)


def _project(x2, g_row, w_l, fb_row, seg_len):
    m = x2.shape[0]
    tm = TILE
    widths = (1024, 1024, 512, 512, 512, 256, 512, LANES, LANES, LANES)
    row = lambda w: pl.BlockSpec((tm, w), lambda i: (i, 0))
    const = lambda shape: pl.BlockSpec(shape, lambda i: (0, 0))
    return pl.pallas_call(
        functools.partial(_proj_kernel, tm=tm, seg_len=seg_len),
        grid=(m // tm,),
        in_specs=[row(D_MODEL), const((1, D_MODEL)), const((D_MODEL, N_IN_PAD)), const((1, LANES))],
        out_specs=[row(w) for w in widths],
        out_shape=[jax.ShapeDtypeStruct((m, w), F32) for w in widths],
        scratch_shapes=[pltpu.VMEM((1, LANES), F32)],
        compiler_params=pltpu.CompilerParams(dimension_semantics=("arbitrary",), vmem_limit_bytes=VMEM_LIMIT),
        name="proj",
    )(x2, g_row, w_l, fb_row)


def _out_kernel(x_ref, oa_ref, ob_ref, oc_ref, od_ref, g4_ref, w_ref, fg_ref, y_ref, *, final):
    g4 = g4_ref[...]
    gate = g4 / (1.0 + jnp.exp(-g4))
    o4 = jnp.concatenate([oa_ref[...], ob_ref[...], oc_ref[...], od_ref[...]], axis=1)
    y = x_ref[...] + _dot((o4 * gate).astype(BF16), w_ref[...])
    if final:
        ms = jnp.mean(y * y, axis=-1, keepdims=True)
        y = (y * lax.rsqrt(ms + RMS_EPS)) * fg_ref[...]
    y_ref[...] = y


def _out_project(x2, outs, g4, w_l, fg_row, final):
    m = x2.shape[0]
    tm = TILE
    row = lambda w: pl.BlockSpec((tm, w), lambda i: (i, 0))
    const = lambda shape: pl.BlockSpec(shape, lambda i: (0, 0))
    return pl.pallas_call(
        functools.partial(_out_kernel, final=final),
        grid=(m // tm,),
        in_specs=[row(D_MODEL)] + [row(BRANCH)] * 4 + [row(D_MODEL), const((D_MODEL, D_MODEL)), const((1, D_MODEL))],
        out_specs=row(D_MODEL),
        out_shape=jax.ShapeDtypeStruct((m, D_MODEL), F32),
        compiler_params=pltpu.CompilerParams(dimension_semantics=("arbitrary",), vmem_limit_bytes=VMEM_LIMIT),
        name="outproj",
    )(x2, *outs, g4, w_l, fg_row)


def _cast_rows(src_ref, dst_ref, n_rows):
    def body(t, _):
        sl = pl.ds(pl.multiple_of(t * TILE, TILE), TILE)
        dst_ref[sl, :] = src_ref[sl, :].astype(BF16)
        return 0
    lax.fori_loop(0, n_rows // TILE, body, 0)


def _flash_multi(qs, kb_ref, vb_ref, acc_ref, i, own_bias, past_bias, row_bias=None):
    n_sub = len(qs)
    r = lax.broadcasted_iota(jnp.int32, (TILE, TILE), 0)
    c = lax.broadcasted_iota(jnp.int32, (TILE, TILE), 1)
    own = pl.ds(pl.multiple_of(i * TILE, TILE), TILE)
    k_own = kb_ref[own, :]
    v_own = vb_ref[own, :]
    stats = []
    for n in range(n_sub):
        s = _dot_nt(qs[n], k_own) + own_bias[n]
        s = jnp.where(c <= r, s, NEG)
        m = jnp.max(s, axis=1, keepdims=True)
        p = jnp.exp2(s - m)
        stats += [m, jnp.sum(p, axis=1, keepdims=True)]
        acc_ref[n] = _dot(p.astype(BF16), v_own)

    def body(j, carry):
        sl = pl.ds(pl.multiple_of(j * TILE, TILE), TILE)
        kt = kb_ref[sl, :]
        vt = vb_ref[sl, :]
        out = []
        for n in range(n_sub):
            s = _dot_nt(qs[n], kt) + past_bias[n](j)
            if row_bias is not None:
                s = s + row_bias[n](j)
            m, alpha, p, l = _online(s, carry[2 * n], carry[2 * n + 1])
            acc_ref[n] = alpha * acc_ref[n] + _dot(p.astype(BF16), vt)
            out += [m, l]
        return tuple(out)

    stats = lax.fori_loop(0, i, body, tuple(stats))
    return [acc_ref[n] / stats[2 * n + 1] for n in range(n_sub)]


def _moba_prompt_kernel(q_ref, k_ref, v_ref, o_ref, kb_ref, vb_ref, km_ref, acc_ref, *, seq, slopes):
    i = pl.program_id(1)
    nb = seq // MOBA_BLOCK

    @pl.when(i == 0)
    def _():
        _cast_rows(k_ref, kb_ref, seq)
        _cast_rows(v_ref, vb_ref, seq)
        km_ref[...] = jnp.zeros_like(km_ref)
        for n in range(nb):
            km_ref[n:n + 1, :] = jnp.mean(k_ref[n * MOBA_BLOCK:(n + 1) * MOBA_BLOCK, :], axis=0, keepdims=True)

    q = q_ref[...]
    lane_head = lax.broadcasted_iota(jnp.int32, (1, BRANCH), 1) // HEAD_DIM
    blk = lax.broadcasted_iota(jnp.int32, (1, LANES), 1)
    blk_f = blk.astype(F32)
    key_f = lax.broadcasted_iota(jnp.int32, (1, TILE), 1).astype(F32)
    scale = LOG2E / math.sqrt(HEAD_DIM)
    qs, own_bias, past_bias, row_bias = [], [], [], []
    for h in range(N_HEADS):
        qf = jnp.where(lane_head == h, q, 0.0)
        gate = _dot_nt(qf, km_ref[...], HIGHEST)
        gate = jnp.where(blk < i, gate, NEG)
        sel_bias = (1.0 - _top_select(gate, MOBA_TOPK, blk_f)) * NEG
        qs.append((qf * scale).astype(BF16))
        slope = float(slopes[h]) * LOG2E
        own_bias.append(slope * key_f)
        past_bias.append(lambda j, slope=slope: slope * ((j - i).astype(F32) * float(TILE) + key_f))
        row_bias.append(lambda j, sel_bias=sel_bias: jnp.sum(jnp.where(blk == j, sel_bias, 0.0), axis=1, keepdims=True))
    res = _flash_multi(qs, kb_ref, vb_ref, acc_ref, i, own_bias, past_bias, row_bias)
    out = res[0]
    for h in range(1, N_HEADS):
        out = jnp.where(lane_head == h, res[h], out)
    o_ref[...] = out


def _diff_prompt_kernel(q_ref, k_ref, v_ref, lam_ref, sg_ref, o_ref, kb_ref, vb_ref, acc_ref, *, seq, slopes, lam_init):
    i = pl.program_id(1)

    @pl.when(i == 0)
    def _():
        _cast_rows(k_ref, kb_ref, seq)
        _cast_rows(v_ref, vb_ref, seq)

    q = q_ref[...]
    lane = lax.broadcasted_iota(jnp.int32, (1, BRANCH), 1)
    key_f = lax.broadcasted_iota(jnp.int32, (1, TILE), 1).astype(F32)
    scale = LOG2E / math.sqrt(DIFF_DIM)
    lam = lam_ref[...]
    qs, own_bias, past_bias = [], [], []
    for sub in range(2 * N_HEADS):
        slope = float(slopes[sub // 2]) * LOG2E
        qs.append((jnp.where((lane // DIFF_DIM) == sub, q, 0.0) * scale).astype(BF16))
        own_bias.append(slope * key_f)
        past_bias.append(lambda j, slope=slope: slope * ((j - i).astype(F32) * float(TILE) + key_f))
    res = _flash_multi(qs, kb_ref, vb_ref, acc_ref, i, own_bias, past_bias)
    out = jnp.zeros((TILE, BRANCH), F32)
    for h in range(N_HEADS):
        mask_h = (lane // HEAD_DIM) == h
        o = jnp.where(mask_h, res[2 * h] - lam * res[2 * h + 1], 0.0)
        ms = jnp.sum(o * o, axis=1, keepdims=True) * (1.0 / HEAD_DIM)
        o = (o * lax.rsqrt(ms + RMS_EPS)) * sg_ref[...] * (1.0 - lam_init)
        out = jnp.where(mask_h, o, out)
    o_ref[...] = out


def _fox_prompt_kernel(q_ref, k_ref, v_ref, ck_ref, o_ref, kb_ref, vb_ref, acc_ref, *, seq):
    i = pl.program_id(1)

    @pl.when(i == 0)
    def _():
        _cast_rows(k_ref, kb_ref, seq)
        _cast_rows(v_ref, vb_ref, seq)

    q = q_ref[...]
    lane_head = lax.broadcasted_iota(jnp.int32, (1, BRANCH), 1) // HEAD_DIM
    scale = LOG2E / math.sqrt(HEAD_DIM)
    own = pl.ds(pl.multiple_of(i * TILE, TILE), TILE)
    qs, own_bias, past_bias = [], [], []
    for h in range(N_HEADS):
        qs.append((jnp.where(lane_head == h, q, 0.0) * scale).astype(BF16))
        ck_own = ck_ref[h:h + 1, own]
        c0 = ck_own[:, 0:1]
        own_bias.append((c0 - ck_own) * LOG2E)

        def bias(j, c0=c0, h=h):
            sl = pl.ds(pl.multiple_of(j * TILE, TILE), TILE)
            return (c0 - ck_ref[h:h + 1, sl]) * LOG2E

        past_bias.append(bias)
    res = _flash_multi(qs, kb_ref, vb_ref, acc_ref, i, own_bias, past_bias)
    out = res[0]
    for h in range(1, N_HEADS):
        out = jnp.where(lane_head == h, res[h], out)
    o_ref[...] = out


def _prompt_call(kern, q4, lane_blk, kv, extras, extra_specs, extra_scratch, name, n_sub=N_HEADS):
    b, seq, _ = kv.shape
    nt = seq // TILE
    in_specs = [
        pl.BlockSpec((None, TILE, BRANCH), lambda bi, i: (bi, i, lane_blk)),
        pl.BlockSpec((None, seq, BRANCH), lambda bi, i: (bi, 0, 0)),
        pl.BlockSpec((None, seq, BRANCH), lambda bi, i: (bi, 0, 1)),
    ] + extra_specs
    return pl.pallas_call(
        kern,
        grid=(b, nt),
        in_specs=in_specs,
        out_specs=pl.BlockSpec((None, TILE, BRANCH), lambda bi, i: (bi, i, 0)),
        out_shape=jax.ShapeDtypeStruct((b, seq, BRANCH), F32),
        scratch_shapes=[pltpu.VMEM((seq, BRANCH), BF16), pltpu.VMEM((seq, BRANCH), BF16)] + extra_scratch
        + [pltpu.VMEM((n_sub, TILE, BRANCH), F32)],
        compiler_params=pltpu.CompilerParams(dimension_semantics=("arbitrary", "arbitrary"),
                                             vmem_limit_bytes=VMEM_LIMIT),
        name=name,
    )(q4, kv, kv, *extras)


def _compress_kernel(x_ref, pe_ref, w1_ref, w2_ref, o_ref, *, nb, nbp):
    def body(s, acc):
        xs = x_ref[pl.ds(s, nb, stride=NSA_BLOCK), :] + pe_ref[pl.ds(s, 1), :]
        return acc + _dot(xs.astype(BF16), w1_ref[s])
    hid = lax.fori_loop(0, NSA_BLOCK, body, jnp.zeros((nb, NSA_GROUPS * NSA_HIDDEN), F32), unroll=8)
    hid = hid / (1.0 + jnp.exp(-hid))
    tok = _dot(hid.astype(BF16), w2_ref[...])
    if nbp > nb:
        tok = jnp.concatenate([tok, jnp.zeros((nbp - nb, LANES), F32)], axis=0)
    o_ref[...] = tok


def _compress(ckv, pe2, w1bd, w2bd, nb, nbp):
    b, length, _ = ckv.shape
    return pl.pallas_call(
        functools.partial(_compress_kernel, nb=nb, nbp=nbp),
        grid=(2, b),
        in_specs=[
            pl.BlockSpec((None, length, LANES), lambda c, bi: (bi, 0, c)),
            pl.BlockSpec((None, NSA_BLOCK, LANES), lambda c, bi: (c, 0, 0)),
            pl.BlockSpec((None, NSA_BLOCK, LANES, NSA_GROUPS * NSA_HIDDEN), lambda c, bi: (c, 0, 0, 0)),
            pl.BlockSpec((None, NSA_GROUPS * NSA_HIDDEN, LANES), lambda c, bi: (c, 0, 0)),
        ],
        out_specs=pl.BlockSpec((None, None, nbp, LANES), lambda c, bi: (bi, c, 0, 0)),
        out_shape=jax.ShapeDtypeStruct((b, 2, nbp, LANES), F32),
        compiler_params=pltpu.CompilerParams(dimension_semantics=("arbitrary", "arbitrary"),
                                             vmem_limit_bytes=VMEM_LIMIT),
        name="nsa_compress",
    )(ckv, pe2, w1bd, w2bd)


def _nsa_kernel(q_ref, sm_ref, tok_ref, sk_ref, sv_ref, wk_ref, wv_ref, eg_ref, o_ref,
                skb_ref, svb_ref, wkb_ref, wvb_ref, *, tq, tk, tkw, sel_len, win_len, nbp, q_off, w_off, prompt, slopes):
    i = pl.program_id(1)

    @pl.when(i == 0)
    def _():
        _cast_rows(sk_ref, skb_ref, sel_len)
        _cast_rows(sv_ref, svb_ref, sel_len)
        _cast_rows(wk_ref, wkb_ref, win_len)
        _cast_rows(wv_ref, wvb_ref, win_len)

    scale = LOG2E / math.sqrt(HEAD_DIM)
    q = q_ref[...] * scale
    lane = lax.broadcasted_iota(jnp.int32, (1, LANES), 1)
    t0 = q_off + i * tq
    t_col = t0 + lax.broadcasted_iota(jnp.int32, (tq, 1), 0)
    t2 = jnp.concatenate([t_col, t_col], axis=0)
    r2 = t2 - t0
    kc = tok_ref[0].astype(BF16)
    vc = tok_ref[1].astype(BF16)
    blk = lax.broadcasted_iota(jnp.int32, (1, nbp), 1)
    blk_f = blk.astype(F32)
    cend = blk * NSA_BLOCK + (NSA_BLOCK - 1)
    key_i = lax.broadcasted_iota(jnp.int32, (1, tk), 1)
    wkey_i = lax.broadcasted_iota(jnp.int32, (1, tkw), 1)
    sig = 1.0 / (1.0 + jnp.exp(-sm_ref[...]))
    gates = [_dot(sig, eg_ref[br], HIGHEST) for br in range(3)]
    n_sel_tiles = (i + 1) if prompt else sel_len // tk
    w_lo = jnp.maximum(i - 2, 0) if prompt else 0
    w_hi = (i + 1) if prompt else win_len // tkw

    blk_col = lax.broadcasted_iota(jnp.int32, (nbp, 1), 0)
    qgbs, slope2s, o_cmps, sels = [], [], [], []
    for g in range(NSA_GROUPS):
        mask_g = (lane // HEAD_DIM) == g
        qg = jnp.concatenate([jnp.where(mask_g, q[:, :LANES], 0.0), jnp.where(mask_g, q[:, LANES:], 0.0)], axis=0)
        qgb = qg.astype(BF16)
        h_a, h_b = C_HEAD_ORDER[g], C_HEAD_ORDER[2 + g]
        slope2 = jnp.concatenate([jnp.full((tq, 1), float(slopes[h_a]) * LOG2E, F32),
                                  jnp.full((tq, 1), float(slopes[h_b]) * LOG2E, F32)], axis=0)
        cdist = t2 - cend
        cmask = cdist >= 0
        s = _dot_nt(qgb, kc) - slope2 * cdist.astype(F32)
        s = jnp.where(cmask, s, NEG)
        m = jnp.max(s, axis=1, keepdims=True)
        m = jnp.where(m > 0.5 * NEG, m, 0.0)
        e = jnp.where(cmask, jnp.exp2(s - m), 0.0)
        den = jnp.sum(e, axis=1, keepdims=True)
        p_cmp = e / jnp.where(den > 0.0, den, 1.0)
        o_cmp = _dot(p_cmp.astype(BF16), vc)
        imp = p_cmp[:tq] + p_cmp[tq:]
        cur = t_col // NSA_BLOCK
        forced = (blk == 0) | (blk == cur) | (blk == cur - 1)
        score = jnp.where(forced, NSA_FORCE_SCORE, imp)
        score = jnp.where(blk <= cur, score, NEG)
        sels.append(_top_select(score, NSA_TOPN, blk_f).astype(BF16))
        qgbs.append(qgb)
        slope2s.append(slope2)
        o_cmps.append(o_cmp)

    def masked_update(s, ok, carry, v_tile):
        m, l, acc = carry
        s = jnp.where(ok, s, NEG)
        m_new = jnp.maximum(m, jnp.max(s, axis=1, keepdims=True))
        alpha = jnp.exp2(m - m_new)
        p = jnp.where(ok, jnp.exp2(s - m_new), 0.0)
        l = alpha * l + jnp.sum(p, axis=1, keepdims=True)
        return m_new, l, alpha * acc + _dot(p.astype(BF16), v_tile)

    def sel_body(j, carry):
        sl = pl.ds(pl.multiple_of(j * tk, tk), tk)
        kpos = j * tk + key_i
        expand = (blk_col == (kpos // NSA_BLOCK)).astype(BF16)
        k_tile, v_tile = skb_ref[sl, :], svb_ref[sl, :]
        rel = (kpos - t0).astype(F32)
        out = []
        for g in range(NSA_GROUPS):
            chosen = _dot(sels[g], expand)
            chosen = jnp.concatenate([chosen, chosen], axis=0)
            ok = (chosen > 0.5) & (kpos <= t2)
            s = _dot_nt(qgbs[g], k_tile) + slope2s[g] * rel
            out += list(masked_update(s, ok, carry[3 * g:3 * g + 3], v_tile))
        return tuple(out)

    def win_body(j, carry):
        sl = pl.ds(pl.multiple_of(j * tkw, tkw), tkw)
        kpos = w_off + j * tkw + wkey_i
        wdist = t2 - kpos
        ok = (wdist >= 0) & (wdist < NSA_WINDOW)
        k_tile, v_tile = wkb_ref[sl, :], wvb_ref[sl, :]
        rel = (kpos - t0).astype(F32)
        out = []
        for g in range(NSA_GROUPS):
            s = _dot_nt(qgbs[g], k_tile) + slope2s[g] * rel
            out += list(masked_update(s, ok, carry[3 * g:3 * g + 3], v_tile))
        return tuple(out)

    init = (jnp.full((2 * tq, 1), NEG, F32), jnp.zeros((2 * tq, 1), F32), jnp.zeros((2 * tq, LANES), F32)) * NSA_GROUPS
    fin = lax.fori_loop(0, n_sel_tiles, sel_body, init)
    o_sels = [fin[3 * g + 2] / jnp.where(fin[3 * g + 1] > 0.0, fin[3 * g + 1], 1.0) for g in range(NSA_GROUPS)]
    fin = lax.fori_loop(w_lo, w_hi, win_body, init)
    o_wins = [fin[3 * g + 2] / jnp.where(fin[3 * g + 1] > 0.0, fin[3 * g + 1], 1.0) for g in range(NSA_GROUPS)]
    res = [(o_cmps[g], o_sels[g], o_wins[g]) for g in range(NSA_GROUPS)]

    low = lane < HEAD_DIM
    out = jnp.zeros((tq, BRANCH), F32)
    for br in range(3):
        first = jnp.where(low, res[0][br][:tq], res[1][br][:tq])
        second = jnp.where(low, res[0][br][tq:], res[1][br][tq:])
        out = out + gates[br] * jnp.concatenate([first, second], axis=1)
    o_ref[...] = out


def _nsa_attend(q_arr, q_lane_blk, small, tok, selkv, win, egate, *, tq, tk, tkw, q_off, w_off, prompt, slopes):
    b, tq_total, _ = small.shape
    sel_len, win_len, nbp = selkv.shape[1], win.shape[1], tok.shape[2]
    nt = tq_total // tq
    assert sel_len % tk == 0 and win_len % tkw == 0 and (not prompt or tq == tk == tkw)
    kern = functools.partial(_nsa_kernel, tq=tq, tk=tk, tkw=tkw, sel_len=sel_len, win_len=win_len, nbp=nbp, q_off=q_off,
                             w_off=w_off, prompt=prompt, slopes=slopes)
    return pl.pallas_call(
        kern,
        grid=(b, nt),
        in_specs=[
            pl.BlockSpec((None, tq, BRANCH), lambda bi, i: (bi, i, q_lane_blk)),
            pl.BlockSpec((None, tq, LANES), lambda bi, i: (bi, i, 0)),
            pl.BlockSpec((None, 2, nbp, LANES), lambda bi, i: (bi, 0, 0, 0)),
            pl.BlockSpec((None, sel_len, LANES), lambda bi, i: (bi, 0, 2)),
            pl.BlockSpec((None, sel_len, LANES), lambda bi, i: (bi, 0, 3)),
            pl.BlockSpec((None, win_len, LANES), lambda bi, i: (bi, 0, 0)),
            pl.BlockSpec((None, win_len, LANES), lambda bi, i: (bi, 0, 1)),
            pl.BlockSpec((3, LANES, BRANCH), lambda bi, i: (0, 0, 0)),
        ],
        out_specs=pl.BlockSpec((None, tq, BRANCH), lambda bi, i: (bi, i, 0)),
        out_shape=jax.ShapeDtypeStruct((b, tq_total, BRANCH), F32),
        scratch_shapes=[pltpu.VMEM((sel_len, LANES), BF16), pltpu.VMEM((sel_len, LANES), BF16),
                        pltpu.VMEM((win_len, LANES), BF16), pltpu.VMEM((win_len, LANES), BF16)],
        compiler_params=pltpu.CompilerParams(dimension_semantics=("arbitrary", "arbitrary"),
                                             vmem_limit_bytes=VMEM_LIMIT),
        name="nsa_prompt" if prompt else "nsa_sample",
    )(q_arr, small, tok, selkv, selkv, win, win, egate)


def _stack_queries(q, n_sub, sub_width, scale):
    lane = lax.broadcasted_iota(jnp.int32, (1, BRANCH), 1) // sub_width
    return jnp.concatenate([jnp.where(lane == s, q, 0.0) * scale for s in range(n_sub)], axis=0)


def _page_scores(qs, pages):
    return jnp.concatenate([_dot(qs, pg[:BRANCH, :].astype(BF16)) for pg in pages], axis=1)


def _page_values(p, pages):
    page_rows = pages[0].shape[1]
    acc = None
    for n, pg in enumerate(pages):
        term = _dot_nt(p[:, n * page_rows:(n + 1) * page_rows].astype(BF16), pg[BRANCH:, :].astype(BF16))
        acc = term if acc is None else acc + term
    return acc


def _pad_rows(x, rows):
    return jnp.concatenate([x, jnp.zeros((rows - x.shape[0], x.shape[1]), x.dtype)], axis=0)


def _stream_kernel(pt_ref, q_ref, new_ref, *rest, mode, n_rows, past, slopes, lam_init):
    pages = rest[:PAGES_PER_STEP]
    rest = rest[PAGES_PER_STEP:]
    if mode == "diff":
        lam_ref, sg_ref, o_ref, qs_ref, m_ref, l_ref, acc_ref = rest
        n_sub, sub_width = 2 * N_HEADS, DIFF_DIM
    else:
        suf_ref, cknew_ref, o_ref, qs_ref, m_ref, l_ref, acc_ref = rest
        n_sub, sub_width = N_HEADS, HEAD_DIM
    c = pl.program_id(1)
    page_rows = pages[0].shape[1]
    step_keys = PAGES_PER_STEP * page_rows
    rows = n_sub * n_rows
    sub_of_row = lax.broadcasted_iota(jnp.int32, (rows, 1), 0) // n_rows

    @pl.when(c == 0)
    def _():
        qs_ref[...] = _stack_queries(q_ref[...], n_sub, sub_width, LOG2E / math.sqrt(sub_width)).astype(BF16)
        m_ref[...] = jnp.full_like(m_ref, NEG)
        l_ref[...] = jnp.zeros_like(l_ref)
        acc_ref[...] = jnp.zeros_like(acc_ref)

    qs = qs_ref[...]
    if mode == "diff":
        slope_col = jnp.zeros((rows, 1), F32)
        for h in range(N_HEADS):
            slope_col = jnp.where(sub_of_row // 2 == h, float(slopes[h]) * LOG2E, slope_col)
    vals = [pg[...] for pg in pages]
    s = _page_scores(qs, vals)
    if mode == "diff":
        rel = (c * step_keys - past + lax.broadcasted_iota(jnp.int32, (1, step_keys), 1)).astype(F32)
        s = s + slope_col * rel
    else:
        suf = suf_ref[...] * LOG2E
        s = s + jnp.concatenate([jnp.broadcast_to(suf[h:h + 1, :], (n_rows, step_keys)) for h in range(N_HEADS)], axis=0)
    m, alpha, p, l = _online(s, m_ref[...], l_ref[...])
    m_ref[...] = m
    l_ref[...] = l
    acc_ref[...] = alpha * acc_ref[...] + _page_values(p, vals)

    @pl.when(c == pl.num_programs(1) - 1)
    def _():
        new = new_ref[...]
        kn = _pad_rows(new[:, :BRANCH], LANES).astype(BF16)
        vn = _pad_rows(new[:, BRANCH:], LANES).astype(BF16)
        col = lax.broadcasted_iota(jnp.int32, (1, LANES), 1)
        qrow = lax.broadcasted_iota(jnp.int32, (rows, 1), 0) % n_rows
        s = _dot_nt(qs, kn)
        if mode == "diff":
            s = s + slope_col * col.astype(F32)
        else:
            ckn = cknew_ref[...] * LOG2E
            s = s - jnp.concatenate([jnp.broadcast_to(ckn[h:h + 1, :], (n_rows, LANES)) for h in range(N_HEADS)], axis=0)
        s = jnp.where(col <= qrow, s, NEG)
        m, alpha, p, l = _online(s, m_ref[...], l_ref[...])
        o = (alpha * acc_ref[...] + _dot(p.astype(BF16), vn)) / l
        lane_head = lax.broadcasted_iota(jnp.int32, (1, BRANCH), 1) // HEAD_DIM
        out = jnp.zeros((n_rows, BRANCH), F32)
        if mode == "diff":
            lam = lam_ref[...]
            for h in range(N_HEADS):
                mask_h = lane_head == h
                a0 = o[(2 * h) * n_rows:(2 * h + 1) * n_rows]
                a1 = o[(2 * h + 1) * n_rows:(2 * h + 2) * n_rows]
                oh = jnp.where(mask_h, a0 - lam * a1, 0.0)
                ms = jnp.sum(oh * oh, axis=1, keepdims=True) * (1.0 / HEAD_DIM)
                oh = (oh * lax.rsqrt(ms + RMS_EPS)) * sg_ref[...] * (1.0 - lam_init)
                out = jnp.where(mask_h, oh, out)
        else:
            for h in range(N_HEADS):
                out = jnp.where(lane_head == h, o[h * n_rows:(h + 1) * n_rows], out)
        o_ref[...] = out


def _moba_sample_kernel(pt_ref, q_ref, new_ref, *rest, n_rows, past, slopes):
    pages = rest[:PAGES_PER_STEP]
    o_ref, qf_ref, kmt_ref, mb_ref, lb_ref, accb_ref = rest[PAGES_PER_STEP:]
    c = pl.program_id(1)
    page_rows = pages[0].shape[1]
    pages_per_block = MOBA_BLOCK // page_rows
    blocks_per_step = PAGES_PER_STEP // pages_per_block
    rows = N_HEADS * n_rows
    head_of_row = lax.broadcasted_iota(jnp.int32, (rows, 1), 0) // n_rows
    lane = lax.broadcasted_iota(jnp.int32, (1, LANES), 1)
    slope_col = jnp.zeros((rows, 1), F32)
    for h in range(N_HEADS):
        slope_col = jnp.where(head_of_row == h, float(slopes[h]) * LOG2E, slope_col)

    @pl.when(c == 0)
    def _():
        qf_ref[...] = _stack_queries(q_ref[...], N_HEADS, HEAD_DIM, 1.0)
        kmt_ref[...] = jnp.zeros_like(kmt_ref)
        mb_ref[...] = jnp.full_like(mb_ref, NEG)
        lb_ref[...] = jnp.zeros_like(lb_ref)

    qf = qf_ref[...]
    qs = (qf * (LOG2E / math.sqrt(HEAD_DIM))).astype(BF16)
    key_f = lax.broadcasted_iota(jnp.int32, (1, MOBA_BLOCK), 1).astype(F32)
    for nb in range(blocks_per_step):
        vals = [pages[nb * pages_per_block + t][...] for t in range(pages_per_block)]
        n = c * blocks_per_step + nb
        ksum = None
        for pg in vals:
            part = jnp.sum(pg[:BRANCH, :], axis=1, keepdims=True)
            ksum = part if ksum is None else ksum + part
        kmt_ref[...] = jnp.where(lane == n, ksum * (1.0 / MOBA_BLOCK), kmt_ref[...])
        rel = (n * MOBA_BLOCK - past).astype(F32) + key_f
        s = _page_scores(qs, vals) + slope_col * rel
        m = jnp.max(s, axis=1, keepdims=True)
        p = jnp.exp2(s - m)
        mb_ref[...] = jnp.where(lane == n, m, mb_ref[...])
        lb_ref[...] = jnp.where(lane == n, jnp.sum(p, axis=1, keepdims=True), lb_ref[...])
        accb_ref[n] = _page_values(p, vals)

    @pl.when(c == pl.num_programs(1) - 1)
    def _():
        n_blocks = past // MOBA_BLOCK
        lane_f = lane.astype(F32)
        gate = _dot(qf, kmt_ref[...], HIGHEST)
        gate = jnp.where(lane < n_blocks, gate, NEG)
        sel = _top_select(gate, MOBA_TOPK, lane_f)
        new = new_ref[...]
        kn = _pad_rows(new[:, :BRANCH], LANES).astype(BF16)
        vn = _pad_rows(new[:, BRANCH:], LANES).astype(BF16)
        qrow = lax.broadcasted_iota(jnp.int32, (rows, 1), 0) % n_rows
        s = _dot_nt(qs, kn) + slope_col * lane_f
        s = jnp.where(lane <= qrow, s, NEG)
        m_own = jnp.max(s, axis=1, keepdims=True)
        mb = jnp.where(sel > 0.5, mb_ref[...], NEG)
        m_tot = jnp.maximum(m_own, jnp.max(mb, axis=1, keepdims=True))
        p = jnp.exp2(s - m_tot)
        wgt = jnp.where(sel > 0.5, jnp.exp2(mb - m_tot), 0.0)
        l = jnp.sum(p, axis=1, keepdims=True) + jnp.sum(wgt * lb_ref[...], axis=1, keepdims=True)
        acc = _dot(p.astype(BF16), vn)
        for n in range(n_blocks):
            acc = acc + wgt[:, n:n + 1] * accb_ref[n]
        o = acc / l
        lane_head = lax.broadcasted_iota(jnp.int32, (1, BRANCH), 1) // HEAD_DIM
        out = jnp.zeros((n_rows, BRANCH), F32)
        for h in range(N_HEADS):
            out = jnp.where(lane_head == h, o[h * n_rows:(h + 1) * n_rows], out)
        o_ref[...] = out


def _sample_call(kern, page_table, layer, q4s, lane_blk, new_kv, cache, extras, extra_specs, scratch, n_rows, name):
    b, n_pages = page_table.shape
    page_rows = cache.shape[3]
    nch = n_pages // PAGES_PER_STEP

    def page_spec(t):
        return pl.BlockSpec((None, None, 2 * BRANCH, page_rows),
                            lambda bi, c, pt: (layer, pt[bi, c * PAGES_PER_STEP + t], 0, 0))

    in_specs = [
        pl.BlockSpec((n_rows, BRANCH), lambda bi, c, pt: (bi, lane_blk)),
        pl.BlockSpec((n_rows, 2 * BRANCH), lambda bi, c, pt: (bi, 0)),
    ] + [page_spec(t) for t in range(PAGES_PER_STEP)] + extra_specs
    grid_spec = pltpu.PrefetchScalarGridSpec(
        num_scalar_prefetch=1,
        grid=(b, nch),
        in_specs=in_specs,
        out_specs=pl.BlockSpec((n_rows, BRANCH), lambda bi, c, pt: (bi, 0)),
        scratch_shapes=scratch,
    )
    return pl.pallas_call(
        kern,
        grid_spec=grid_spec,
        out_shape=jax.ShapeDtypeStruct((b * n_rows, BRANCH), F32),
        compiler_params=pltpu.CompilerParams(dimension_semantics=("arbitrary", "arbitrary"),
                                             vmem_limit_bytes=VMEM_LIMIT),
        name=name,
    )(page_table, q4s, new_kv, *([cache] * PAGES_PER_STEP), *extras)


def _suffix_kernel(x_ref, o_ref):
    n_pages, width = x_ref.shape
    x = x_ref[...]
    src = lax.broadcasted_iota(jnp.int32, (width, width), 0)
    dst = lax.broadcasted_iota(jnp.int32, (width, width), 1)
    same_head = (src // LANES) == (dst // LANES)
    later = (src % LANES) > (dst % LANES)
    within = _dot(x, (same_head & later).astype(F32), HIGHEST)
    total = _dot(x, same_head.astype(F32), HIGHEST)
    pr = lax.broadcasted_iota(jnp.int32, (n_pages, n_pages), 0)
    pc = lax.broadcasted_iota(jnp.int32, (n_pages, n_pages), 1)
    o_ref[...] = within + _dot((pc > pr).astype(F32), total, HIGHEST)


def _suffix_sums(logf_pages):
    b, n_pages, width = logf_pages.shape
    return pl.pallas_call(
        _suffix_kernel,
        grid=(b,),
        in_specs=[pl.BlockSpec((None, n_pages, width), lambda bi: (bi, 0, 0))],
        out_specs=pl.BlockSpec((None, n_pages, width), lambda bi: (bi, 0, 0)),
        out_shape=jax.ShapeDtypeStruct((b, n_pages, width), F32),
        compiler_params=pltpu.CompilerParams(dimension_semantics=("arbitrary",)),
        name="fox_suffix",
    )(logf_pages)


def _rows_kernel(pt_ref, new_ref, *rest, n_rows):
    pages = rest[:PAGES_PER_STEP]
    o_ref = rest[PAGES_PER_STEP]
    c = pl.program_id(1)
    page_rows = pages[0].shape[1]

    @pl.when(c < pl.num_programs(1) - 1)
    def _():
        for t, pg in enumerate(pages):
            o_ref[t * page_rows:(t + 1) * page_rows, :] = pg[...].T

    @pl.when(c == pl.num_programs(1) - 1)
    def _():
        o_ref[...] = jnp.zeros_like(o_ref)
        o_ref[0:n_rows, :] = new_ref[...]


def _sequence_rows(page_table, layer, new_rows, cache, n_rows):
    b, n_pages = page_table.shape
    page_rows = cache.shape[3]
    nch = n_pages // PAGES_PER_STEP
    chunk = PAGES_PER_STEP * page_rows

    def page_spec(t):
        return pl.BlockSpec((None, None, 2 * BRANCH, page_rows),
                            lambda bi, c, pt: (layer, pt[bi, jnp.minimum(c, nch - 1) * PAGES_PER_STEP + t], 0, 0))

    grid_spec = pltpu.PrefetchScalarGridSpec(
        num_scalar_prefetch=1,
        grid=(b, nch + 1),
        in_specs=[pl.BlockSpec((n_rows, 2 * BRANCH), lambda bi, c, pt: (bi, 0))] + [page_spec(t) for t in range(PAGES_PER_STEP)],
        out_specs=pl.BlockSpec((None, chunk, 2 * BRANCH), lambda bi, c, pt: (bi, c, 0)),
    )
    return pl.pallas_call(
        functools.partial(_rows_kernel, n_rows=n_rows),
        grid_spec=grid_spec,
        out_shape=jax.ShapeDtypeStruct((b, (nch + 1) * chunk, 2 * BRANCH), F32),
        compiler_params=pltpu.CompilerParams(dimension_semantics=("arbitrary", "arbitrary"),
                                             vmem_limit_bytes=VMEM_LIMIT),
        name="nsa_rows",
    )(page_table, new_rows, *([cache] * PAGES_PER_STEP))


def _heads_to_rows(x, lo):
    y = jnp.swapaxes(x[:, :, lo:lo + N_HEADS], 1, 2)
    return jnp.pad(y, ((0, 0), (0, 8 - N_HEADS), (0, 0)))


def kernel(x_prompt, x_sample, cache_a_kv, cache_b_kv, cache_c_kv, cache_c_win, cache_d_kv, cache_d_logf, page_table,
           norm_g, w_in, w_out, diff_lambda, diff_subln_g, nsa_pe, nsa_phi_w1, nsa_phi_w2, fox_bias, final_g):
    depth = w_in.shape[0]
    bp, seq, _ = x_prompt.shape
    bs, dec, _ = x_sample.shape
    n_pool, page_rows = cache_a_kv.shape[1], cache_a_kv.shape[2]
    n_pages = page_table.shape[1]
    past = n_pages * page_rows
    win_c = cache_c_win.shape[2]
    slopes = _alibi_slopes()
    assert seq % TILE == 0 and (bs * dec) % TILE == 0 and dec == 8 and n_pages % PAGES_PER_STEP == 0
    assert past % MOBA_BLOCK == 0 and MOBA_BLOCK % page_rows == 0 and page_rows == LANES

    perm = _in_perm()
    w_in_p = jnp.pad(w_in[:, :, perm], ((0, 0), (0, 0), (0, N_IN_PAD - N_IN_ORIG))).astype(BF16)
    hp = np.array(C_HEAD_ORDER)
    crow = 2 * BRANCH + (hp[:, None] * HEAD_DIM + np.arange(HEAD_DIM)[None, :]).reshape(-1)
    out_rows = np.concatenate([np.arange(2 * BRANCH), crow, np.arange(3 * BRANCH, 4 * BRANCH)])
    w_out_p = w_out[:, out_rows, :].astype(BF16)
    fb = jnp.zeros((depth, 1, LANES), F32).at[:, 0, SMALL_F:SMALL_F + N_HEADS].set(fox_bias)
    norm_rows = norm_g.reshape(depth, 1, D_MODEL)
    final_row = final_g.reshape(1, D_MODEL)
    lam_v = diff_lambda.astype(F32)
    lam_dyn = jnp.exp(jnp.sum(lam_v[:, 0] * lam_v[:, 1], axis=-1)) - jnp.exp(jnp.sum(lam_v[:, 2] * lam_v[:, 3], axis=-1))
    subln = jnp.tile(diff_subln_g, (1, N_HEADS)).reshape(depth, 1, BRANCH)
    egate = np.zeros((3, LANES, BRANCH), np.float32)
    for br in range(3):
        for pos, h in enumerate(C_HEAD_ORDER):
            egate[br, h * 3 + br, pos * HEAD_DIM:(pos + 1) * HEAD_DIM] = 1.0
    egate = jnp.asarray(egate)
    pe2 = jnp.tile(nsa_pe, (1, 1, 1, NSA_GROUPS))
    w1 = nsa_phi_w1.reshape(depth, 2, NSA_BLOCK, HEAD_DIM, NSA_HIDDEN)
    z1 = jnp.zeros_like(w1)
    w1bd = jnp.concatenate([jnp.concatenate([w1, z1], axis=-1), jnp.concatenate([z1, w1], axis=-1)], axis=-2).astype(BF16)
    z2 = jnp.zeros_like(nsa_phi_w2)
    w2bd = jnp.concatenate([jnp.concatenate([nsa_phi_w2, z2], axis=-1),
                            jnp.concatenate([z2, nsa_phi_w2], axis=-1)], axis=-2).astype(BF16)

    pages_fm = lambda c: jnp.transpose(c, (0, 1, 3, 4, 5, 2)).reshape(depth, n_pool, 2 * BRANCH, page_rows)
    cache_a, cache_b, cache_c, cache_d = map(pages_fm, (cache_a_kv, cache_b_kv, cache_c_kv, cache_d_kv))
    cache_w = cache_c_win.reshape(depth, bs, win_c, BRANCH)
    cache_f = jnp.swapaxes(cache_d_logf, 2, 3).reshape(depth, n_pool, N_HEADS * page_rows)

    nb_p = seq // NSA_BLOCK
    nbp_p = -(-nb_p // LANES) * LANES
    step_keys = PAGES_PER_STEP * page_rows
    sel_len_s = past + step_keys
    sel_tile_s = sel_len_s // 4
    assert sel_tile_s % LANES == 0 and sel_len_s % TILE == 0
    nb_s = sel_len_s // NSA_BLOCK
    nbp_s = -(-nb_s // LANES) * LANES
    win_len_s = -(-(win_c + dec) // TILE) * TILE

    xp = x_prompt.reshape(bp * seq, D_MODEL)
    xs = x_sample.reshape(bs * dec, D_MODEL)
    outs_p = [[] for _ in range(6)]
    outs_s = [[] for _ in range(6)]
    kv_scratch = lambda rows: [pltpu.VMEM((rows, 1), F32), pltpu.VMEM((rows, 1), F32), pltpu.VMEM((rows, BRANCH), F32)]
    const2 = lambda shape: pl.BlockSpec(shape, lambda bi, i: (0, 0))
    const3 = lambda shape: pl.BlockSpec(shape, lambda bi, c, pt: (0, 0))

    for l in range(depth):
        lam_init = 0.8 - 0.6 * math.exp(-0.3 * l)
        lam = (lam_dyn[l] + lam_init).reshape(1, 1)
        sg = subln[l]

        q4, g4, akv, bkv, ckv, cwin, dkv, small, logf, cum = _project(xp, norm_rows[l], w_in_p[l], fb[l], seq)
        r3 = lambda a: a.reshape(bp, seq, a.shape[-1])
        q4, akv, bkv, ckv, cwin, dkv, small3, cum3 = map(r3, (q4, akv, bkv, ckv, cwin, dkv, small, cum))
        o_a = _prompt_call(functools.partial(_moba_prompt_kernel, seq=seq, slopes=slopes[0]), q4, 0, akv,
                           [], [], [pltpu.VMEM((LANES, BRANCH), F32)], "moba_prompt")
        o_b = _prompt_call(functools.partial(_diff_prompt_kernel, seq=seq, slopes=slopes[1], lam_init=lam_init),
                           q4, 1, bkv, [lam, sg], [const2((1, 1)), const2((1, BRANCH))], [], "diff_prompt",
                           n_sub=2 * N_HEADS)
        tok = _compress(ckv, pe2[l], w1bd[l], w2bd[l], nb_p, nbp_p)
        o_c = _nsa_attend(q4, 2, small3, tok, ckv, cwin, egate, tq=TILE, tk=TILE, tkw=TILE, q_off=0, w_off=0, prompt=True,
                          slopes=slopes[2])
        ck_rows = _heads_to_rows(cum3, SMALL_F)
        o_d = _prompt_call(functools.partial(_fox_prompt_kernel, seq=seq), q4, 3, dkv, [ck_rows],
                           [pl.BlockSpec((None, 8, seq), lambda bi, i: (bi, 0, 0))], [], "fox_prompt")
        flat = lambda a: a.reshape(bp * seq, BRANCH)
        xp = _out_project(xp, [flat(o_a), flat(o_b), flat(o_c), flat(o_d)], g4, w_out_p[l], final_row, l == depth - 1)
        new_p = (akv, bkv, ckv, cwin[:, seq - min(NSA_WINDOW, seq):], dkv, logf.reshape(bp, seq, LANES)[:, :, SMALL_F:SMALL_F + N_HEADS])
        for k in range(6):
            outs_p[k].append(new_p[k])

        q4s, g4s, akv_s, bkv_s, ckv_s, cwin_s, dkv_s, small_s, logf_s, cum_s = _project(xs, norm_rows[l], w_in_p[l], fb[l], dec)
        o_a = _sample_call(functools.partial(_moba_sample_kernel, n_rows=dec, past=past, slopes=slopes[0]),
                           page_table, l, q4s, 0, akv_s, cache_a, [], [],
                           [pltpu.VMEM((N_HEADS * dec, BRANCH), F32), pltpu.VMEM((BRANCH, LANES), F32),
                            pltpu.VMEM((N_HEADS * dec, LANES), F32), pltpu.VMEM((N_HEADS * dec, LANES), F32),
                            pltpu.VMEM((past // MOBA_BLOCK, N_HEADS * dec, BRANCH), F32)], dec, "moba_sample")
        o_b = _sample_call(functools.partial(_stream_kernel, mode="diff", n_rows=dec, past=past, slopes=slopes[1], lam_init=lam_init),
                           page_table, l, q4s, 1, bkv_s, cache_b, [lam, sg], [const3((1, 1)), const3((1, BRANCH))],
                           [pltpu.VMEM((2 * N_HEADS * dec, BRANCH), BF16)] + kv_scratch(2 * N_HEADS * dec), dec, "diff_sample")
        suf = _suffix_sums(cache_f[l][page_table])
        suf = suf.reshape(bs, n_pages, N_HEADS, page_rows).transpose(0, 2, 1, 3).reshape(bs, N_HEADS, past)
        suf = jnp.pad(suf, ((0, 0), (0, 8 - N_HEADS), (0, 0)))
        ck_new = jnp.pad(_heads_to_rows(cum_s.reshape(bs, dec, LANES), SMALL_F), ((0, 0), (0, 0), (0, LANES - dec)))
        o_d = _sample_call(functools.partial(_stream_kernel, mode="fox", n_rows=dec, past=past, slopes=None, lam_init=None),
                           page_table, l, q4s, 3, dkv_s, cache_d, [suf, ck_new],
                           [pl.BlockSpec((None, 8, step_keys), lambda bi, c, pt: (bi, 0, c)),
                            pl.BlockSpec((None, 8, LANES), lambda bi, c, pt: (bi, 0, 0))],
                           [pltpu.VMEM((N_HEADS * dec, BRANCH), BF16)] + kv_scratch(N_HEADS * dec), dec, "fox_sample")
        c_all = _sequence_rows(page_table, l, ckv_s, cache_c, dec)
        w_all = jnp.concatenate([cache_w[l], cwin_s.reshape(bs, dec, BRANCH)], axis=1)
        w_pad = jnp.pad(w_all, ((0, 0), (0, win_len_s - win_c - dec), (0, 0)))
        tok_s = _compress(c_all, pe2[l], w1bd[l], w2bd[l], nb_s, nbp_s)
        o_c = _nsa_attend(q4s.reshape(bs, dec, 4 * BRANCH), 2, small_s.reshape(bs, dec, LANES), tok_s, c_all, w_pad, egate,
                          tq=dec, tk=sel_tile_s, tkw=win_len_s, q_off=past, w_off=past - win_c, prompt=False, slopes=slopes[2])
        xs = _out_project(xs, [o_a, o_b, o_c.reshape(bs * dec, BRANCH), o_d], g4s, w_out_p[l], final_row, l == depth - 1)
        r3s = lambda a: a.reshape(bs, dec, a.shape[-1])
        new_s = (r3s(akv_s), r3s(bkv_s), r3s(ckv_s), w_all[:, dec:], r3s(dkv_s), r3s(logf_s)[:, :, SMALL_F:SMALL_F + N_HEADS])
        for k in range(6):
            outs_s[k].append(new_s[k])

    st = lambda parts, shape: jnp.stack(parts, axis=0).reshape((depth,) + shape)
    g = NSA_GROUPS
    y_prompt = xp.reshape(bp, seq, D_MODEL)
    y_sample = xs.reshape(bs, dec, D_MODEL)
    return (
        y_prompt, y_sample,
        st(outs_p[0], (bp, seq, 2, N_HEADS, HEAD_DIM)), st(outs_s[0], (bs, dec, 2, N_HEADS, HEAD_DIM)),
        st(outs_p[1], (bp, seq, 2, N_HEADS, HEAD_DIM)), st(outs_s[1], (bs, dec, 2, N_HEADS, HEAD_DIM)),
        st(outs_p[2], (bp, seq, 4, g, HEAD_DIM)), st(outs_s[2], (bs, dec, 4, g, HEAD_DIM)),
        st(outs_p[3], (bp, min(NSA_WINDOW, seq), 2, g, HEAD_DIM)), st(outs_s[3], (bs, win_c, 2, g, HEAD_DIM)),
        st(outs_p[4], (bp, seq, 2, N_HEADS, HEAD_DIM)), st(outs_s[4], (bs, dec, 2, N_HEADS, HEAD_DIM)),
        st(outs_p[5], (bp, seq, N_HEADS)), st(outs_s[5], (bs, dec, N_HEADS)),
    )
```
